```python
import jax, jax.numpy as jnp
from jax import lax
import numpy as np

D_MODEL = 2048
BATCH = 1
SEQ = 8192
DEPTH = 4

N_BRANCH = 4
BRANCH_W = D_MODEL // N_BRANCH
GROUP_DIM = 128
N_GROUPS = BRANCH_W // GROUP_DIM
CHUNK = 128
CONV_W = 4
LRU_C = 8.0
FFN_HIDDEN = -(-8 * D_MODEL // (3 * 256)) * 256
IN_W = 11 * BRANCH_W + 2 * N_GROUPS
EPS = 1e-6

kernel_name = "hybrid_gated_gmlp_rglru_mlstm_stickbreak"


def rms_norm(x, g):
    xf = x.astype(jnp.float32)
    y = xf * lax.rsqrt(jnp.mean(xf * xf, axis=-1, keepdims=True) + EPS)
    return (y * g.astype(jnp.float32)).astype(x.dtype)


def causal_conv(x, w, b):
    s = x.shape[1]
    xp = jnp.pad(x, ((0, 0), (CONV_W - 1, 0), (0, 0)))
    return sum(xp[:, k:k + s] * w[k] for k in range(CONV_W)) + b


def gmlp_mixer(u_pre, v_pre, g_v, w_s, b_s):
    u = jax.nn.gelu(u_pre)
    v = rms_norm(jax.nn.gelu(v_pre), g_v)
    b, s, _ = v.shape
    nc = s // CHUNK
    vr = v.reshape(b, nc, CHUNK, N_GROUPS, GROUP_DIM)
    tri = jnp.tril(jnp.ones((CHUNK, CHUNK), dtype=bool))
    ws = jnp.where(tri, w_s, 0.0)
    sp = jnp.einsum("gtp,bnpgc->bntgc", ws, vr) + b_s.T[:, :, None]
    return u * sp.reshape(b, s, BRANCH_W)


def _lru_combine(left, right):
    a_l, b_l = left
    a_r, b_r = right
    return a_l * a_r, a_r * b_l + b_r


def rglru_mixer(x_pre, gate_pre, conv_w, conv_b, w_r, b_r, w_i, b_i, lam):
    xc = causal_conv(x_pre, conv_w, conv_b)
    b, s, _ = xc.shape
    xg = xc.reshape(b, s, N_GROUPS, GROUP_DIM)
    r = jax.nn.sigmoid(jnp.einsum("bsgc,gcd->bsgd", xg, w_r).reshape(b, s, BRANCH_W) + b_r)
    i = jax.nn.sigmoid(jnp.einsum("bsgc,gcd->bsgd", xg, w_i).reshape(b, s, BRANCH_W) + b_i)
    log_a = LRU_C * r.astype(jnp.float32) * jax.nn.log_sigmoid(lam.astype(jnp.float32))
    a = jnp.exp(log_a)
    bx = jnp.sqrt(-jnp.expm1(2.0 * log_a)) * (i * xc).astype(jnp.float32)
    _, h = lax.associative_scan(_lru_combine, (a, bx), axis=1)
    return h.astype(x_pre.dtype) * jax.nn.gelu(gate_pre)


def mlstm_mixer(q_pre, k_pre, v_pre, o_pre, i_pre, f_pre, conv_w, conv_b, b_i, b_f, g_h):
    qk = jax.nn.silu(causal_conv(jnp.concatenate([q_pre, k_pre], axis=-1), conv_w, conv_b))
    q, k = jnp.split(qk, 2, axis=-1)
    b, s, _ = q.shape
    nc = s // CHUNK
    f32 = jnp.float32
    qh = q.reshape(b, s, N_GROUPS, GROUP_DIM).astype(f32)
    kh = k.reshape(b, s, N_GROUPS, GROUP_DIM).astype(f32) * (GROUP_DIM ** -0.5)
    vh = v_pre.reshape(b, s, N_GROUPS, GROUP_DIM).astype(f32)
    ig = (i_pre + b_i).astype(f32)
    lf = jax.nn.log_sigmoid((f_pre + b_f).astype(f32))

    def to_chunks(t):
        return jnp.moveaxis(t.reshape((b, nc, CHUNK) + t.shape[2:]), 1, 0)

    tri = jnp.tril(jnp.ones((CHUNK, CHUNK), dtype=bool))

    def step(carry, inp):
        c_st, n_st, m_st = carry
        qc, kc, vc, ic, fc = inp
        ic = ic.transpose(0, 2, 1)
        bc = jnp.cumsum(fc.transpose(0, 2, 1), axis=-1)
        dmat = jnp.where(tri, bc[..., :, None] - bc[..., None, :] + ic[..., None, :], -jnp.inf)
        m_inter = bc + m_st[..., None]
        m_t = jnp.maximum(m_inter, jnp.max(dmat, axis=-1))
        p = jnp.einsum("bthd,bshd->bhts", qc, kc) * jnp.exp(dmat - m_t[..., None])
        sc = jnp.exp(m_inter - m_t)
        num = jnp.einsum("bhts,bshd->bhtd", p, vc) + sc[..., None] * jnp.einsum("bhvk,bthk->bhtv", c_st, qc)
        den = jnp.sum(p, axis=-1) + sc * jnp.einsum("bhk,bthk->bht", n_st, qc)
        h = num / jnp.maximum(jnp.abs(den), jnp.exp(-m_t))[..., None]
        b_last = bc[..., -1]
        g = b_last[..., None] - bc + ic
        m_new = jnp.maximum(b_last + m_st, jnp.max(g, axis=-1))
        w = jnp.exp(g - m_new[..., None])
        decay = jnp.exp(b_last + m_st - m_new)
        c_new = decay[..., None, None] * c_st + jnp.einsum("bhs,bshv,bshk->bhvk", w, vc, kc)
        n_new = decay[..., None] * n_st + jnp.einsum("bhs,bshk->bhk", w, kc)
        return (c_new, n_new, m_new), h.transpose(0, 2, 1, 3)

    init = (jnp.zeros((b, N_GROUPS, GROUP_DIM, GROUP_DIM), f32),
            jnp.zeros((b, N_GROUPS, GROUP_DIM), f32),
            jnp.zeros((b, N_GROUPS), f32))
    _, hs = lax.scan(step, init, (to_chunks(qh), to_chunks(kh), to_chunks(vh), to_chunks(ig), to_chunks(lf)))
    h = jnp.moveaxis(hs, 0, 1).reshape(b, s, N_GROUPS, GROUP_DIM)
    h = rms_norm(h, g_h).reshape(b, s, BRANCH_W).astype(v_pre.dtype)
    return jax.nn.sigmoid(o_pre) * h


def stick_breaking_mixer(q_pre, k_pre, v_pre, g_q, g_k):
    b, s, _ = q_pre.shape
    f32 = jnp.float32
    q = rms_norm(q_pre.reshape(b, s, N_GROUPS, GROUP_DIM), g_q).astype(f32)
    k = rms_norm(k_pre.reshape(b, s, N_GROUPS, GROUP_DIM), g_k).astype(f32)
    v = v_pre.reshape(b, s, N_GROUPS, GROUP_DIM).astype(f32)
    nb = s // CHUNK
    qb = jnp.moveaxis(q.reshape(b, nb, CHUNK, N_GROUPS, GROUP_DIM), 1, 0)
    s_pos = jnp.arange(s)
    scale = GROUP_DIM ** -0.5

    def block(args):
        blk, qc = args
        z = jnp.einsum("bthd,bshd->bhts", qc, k) * scale
        t_pos = blk * CHUNK + jnp.arange(CHUNK)
        strict = s_pos[None, :] < t_pos[:, None]
        ln1m = jnp.where(strict, jax.nn.log_sigmoid(-z), 0.0)
        rem = lax.cumsum(ln1m, axis=3, reverse=True) - ln1m
        att = jnp.where(strict, jnp.exp(jax.nn.log_sigmoid(z) + rem), 0.0)
        return jnp.einsum("bhts,bshd->bthd", att, v)

    out = lax.map(block, (jnp.arange(nb), qb))
    return jnp.moveaxis(out, 0, 1).reshape(b, s, BRANCH_W).astype(v_pre.dtype)


def setup_inputs(seed: int = 0) -> dict:
    key = jax.random.key(seed)
    ks = jax.random.split(key, 32)
    f32 = jnp.float32
    L, W, G, GD, D, F = DEPTH, BRANCH_W, N_GROUPS, GROUP_DIM, D_MODEL, FFN_HIDDEN

    def nrm(k, shape, scale):
        return jax.random.normal(k, shape, f32) * scale

    def gain(k, shape):
        return 1.0 + 0.02 * jax.random.normal(k, shape, f32)

    a8 = jax.random.uniform(ks[12], (L, W), f32, 0.9, 0.999)
    a = a8 ** (1.0 / LRU_C)
    lam = jnp.log(a) - jnp.log1p(-a)
    return {
        "x": nrm(ks[0], (BATCH, SEQ, D), 1.0),
        "norm_mix": gain(ks[1], (L, D)),
        "w_in": nrm(ks[2], (L, D, IN_W), D ** -0.5),
        "gm_norm_v": gain(ks[3], (L, W)),
        "gm_w_s": nrm(ks[4], (L, G, CHUNK, CHUNK), 0.5 * CHUNK ** -0.5),
        "gm_b_s": 1.0 + 0.1 * jax.random.normal(ks[5], (L, G, CHUNK), f32),
        "lru_conv_w": nrm(ks[6], (L, CONV_W, W), CONV_W ** -0.5),
        "lru_conv_b": nrm(ks[7], (L, W), 0.01),
        "lru_w_r": nrm(ks[8], (L, G, GD, GD), GD ** -0.5),
        "lru_b_r": nrm(ks[9], (L, W), 0.01),
        "lru_w_i": nrm(ks[10], (L, G, GD, GD), GD ** -0.5),
        "lru_b_i": nrm(ks[11], (L, W), 0.01),
        "lru_lambda": lam,
        "ml_conv_w": nrm(ks[13], (L, CONV_W, 2 * W), CONV_W ** -0.5),
        "ml_conv_b": nrm(ks[14], (L, 2 * W), 0.01),
        "ml_b_i": -2.0 + nrm(ks[15], (L, G), 0.1),
        "ml_b_f": jnp.linspace(3.0, 6.0, G, dtype=f32)[None, :] + nrm(ks[16], (L, G), 0.1),
        "ml_norm_h": gain(ks[17], (L, G, GD)),
        "sb_norm_q": gain(ks[18], (L, GD)),
        "sb_norm_k": gain(ks[19], (L, GD)),
        "w_branch": nrm(ks[20], (L, N_BRANCH, W, D), W ** -0.5),
        "w_gate": nrm(ks[21], (L, N_BRANCH, D, D), D ** -0.5),
        "b_gate": nrm(ks[22], (L, N_BRANCH, D), 0.01),
        "w_out": nrm(ks[23], (L, D, D), D ** -0.5),
        "norm_ffn": gain(ks[24], (L, D)),
        "w_ffn_gate": nrm(ks[25], (L, D, F), D ** -0.5),
        "w_ffn_up": nrm(ks[26], (L, D, F), D ** -0.5),
        "w_ffn_down": nrm(ks[27], (L, F, D), F ** -0.5),
    }


def reference(x, norm_mix, w_in, gm_norm_v, gm_w_s, gm_b_s, lru_conv_w, lru_conv_b, lru_w_r, lru_b_r,
              lru_w_i, lru_b_i, lru_lambda, ml_conv_w, ml_conv_b, ml_b_i, ml_b_f, ml_norm_h, sb_norm_q,
              sb_norm_k, w_branch, w_gate, b_gate, w_out, norm_ffn, w_ffn_gate, w_ffn_up, w_ffn_down):
    sizes = (BRANCH_W,) * 8 + (N_GROUPS,) * 2 + (BRANCH_W,) * 3
    offsets = np.cumsum(sizes)[:-1].tolist()
    for l in range(DEPTH):
        xn = rms_norm(x, norm_mix[l])
        proj = xn @ w_in[l]
        (a_u, a_v, b_x, b_g, c_q, c_k, c_v, c_o, c_i, c_f, d_q, d_k, d_v) = jnp.split(proj, offsets, axis=-1)
        y_a = gmlp_mixer(a_u, a_v, gm_norm_v[l], gm_w_s[l], gm_b_s[l])
        y_b = rglru_mixer(b_x, b_g, lru_conv_w[l], lru_conv_b[l], lru_w_r[l], lru_b_r[l],
                          lru_w_i[l], lru_b_i[l], lru_lambda[l])
        y_c = mlstm_mixer(c_q, c_k, c_v, c_o, c_i, c_f, ml_conv_w[l], ml_conv_b[l],
                          ml_b_i[l], ml_b_f[l], ml_norm_h[l])
        y_d = stick_breaking_mixer(d_q, d_k, d_v, sb_norm_q[l], sb_norm_k[l])
        branches = (y_a, y_b, y_c, y_d)
        merged = sum(jax.nn.sigmoid(xn @ w_gate[l, g] + b_gate[l, g]) * (branches[g] @ w_branch[l, g])
                     for g in range(N_BRANCH))
        x = x + merged @ w_out[l]
        hn = rms_norm(x, norm_ffn[l])
        x = x + (jax.nn.silu(hn @ w_ffn_gate[l]) * (hn @ w_ffn_up[l])) @ w_ffn_down[l]
    return x
```

```python
import functools

import jax
import jax.numpy as jnp
from jax import lax
from jax.experimental import pallas as pl
from jax.experimental.pallas import tpu as pltpu

D_MODEL = 2048
N_BRANCH = 4
BRANCH_W = D_MODEL // N_BRANCH
GROUP_DIM = 128
N_GROUPS = BRANCH_W // GROUP_DIM
CHUNK = 128
CONV_W = 4
LRU_C = 8.0
EPS = 1e-6
IF_PAD = 128
IF_ROWS = 16

F32 = jnp.float32
BF16 = jnp.bfloat16

VMEM_LIMIT_BYTES = 48 * 1024 * 1024

(COL_A_U, COL_A_V, COL_B_X, COL_B_G, COL_C_Q, COL_C_K, COL_C_V, COL_C_O,
 COL_D_Q, COL_D_K, COL_D_V) = range(11)


def _params(*sem):
    return pltpu.CompilerParams(dimension_semantics=sem, vmem_limit_bytes=VMEM_LIMIT_BYTES)


def _dot(a, b):
    return jnp.dot(a, b, preferred_element_type=F32)


def _dot_nt(a, b):
    return lax.dot_general(a, b, (((1,), (1,)), ((), ())), preferred_element_type=F32)


def _log_sigmoid(x):
    return jnp.minimum(x, 0.0) - jnp.log1p(jnp.exp(-jnp.abs(x)))


def _rms(x):
    return x * lax.rsqrt(jnp.mean(x * x, axis=-1, keepdims=True) + EPS)


def _split3(a):
    a1 = a.astype(BF16)
    r1 = a - a1.astype(F32)
    a2 = r1.astype(BF16)
    a3 = (r1 - a2.astype(F32)).astype(BF16)
    return a1, a2, a3


def _norm_if_kernel(x_ref, g_ref, wif_ref, wift_ref, xn_ref, ifc_ref, ifr_ref):
    xn = (_rms(x_ref[...]) * g_ref[...]).astype(BF16)
    xn_ref[...] = xn
    ifc_ref[...] = _dot(xn, wif_ref[...])
    ifr_ref[...] = _dot_nt(wift_ref[...], xn)


def _norm_kernel(x_ref, g_ref, xn_ref):
    xn_ref[...] = (_rms(x_ref[...]) * g_ref[...]).astype(BF16)


def _norm_if(x, g, wif, wift, tm):
    s, d = x.shape
    return pl.pallas_call(
        _norm_if_kernel,
        grid=(s // tm,),
        in_specs=[pl.BlockSpec((tm, d), lambda i: (i, 0)),
                  pl.BlockSpec((1, d), lambda i: (0, 0)),
                  pl.BlockSpec((d, IF_PAD), lambda i: (0, 0)),
                  pl.BlockSpec((IF_ROWS, d), lambda i: (0, 0))],
        out_specs=[pl.BlockSpec((tm, d), lambda i: (i, 0)),
                   pl.BlockSpec((tm, IF_PAD), lambda i: (i, 0)),
                   pl.BlockSpec((IF_ROWS, tm), lambda i: (0, i))],
        out_shape=[jax.ShapeDtypeStruct((s, d), BF16),
                   jax.ShapeDtypeStruct((s, IF_PAD), F32),
                   jax.ShapeDtypeStruct((IF_ROWS, s), F32)],
        compiler_params=_params("parallel"),
        name="norm_if",
    )(x, g, wif, wift)


def _norm(x, g, tm):
    s, d = x.shape
    return pl.pallas_call(
        _norm_kernel,
        grid=(s // tm,),
        in_specs=[pl.BlockSpec((tm, d), lambda i: (i, 0)),
                  pl.BlockSpec((1, d), lambda i: (0, 0))],
        out_specs=pl.BlockSpec((tm, d), lambda i: (i, 0)),
        out_shape=jax.ShapeDtypeStruct((s, d), BF16),
        compiler_params=_params("parallel"),
        name="norm",
    )(x, g)


def _mm_kernel(x_ref, w_ref, o_ref):
    o_ref[...] = _dot(x_ref[...], w_ref[...]).astype(o_ref.dtype)


def _mm_res_kernel(x_ref, w_ref, r_ref, o_ref):
    o_ref[...] = r_ref[...] + _dot(x_ref[...], w_ref[...])


def _matmul(x, w, out_dtype, tm, tn, name):
    m, k = x.shape
    n = w.shape[1]
    return pl.pallas_call(
        _mm_kernel,
        grid=(m // tm, n // tn),
        in_specs=[pl.BlockSpec((tm, k), lambda i, j: (i, 0)),
                  pl.BlockSpec((k, tn), lambda i, j: (0, j))],
        out_specs=pl.BlockSpec((tm, tn), lambda i, j: (i, j)),
        out_shape=jax.ShapeDtypeStruct((m, n), out_dtype),
        compiler_params=_params("parallel", "parallel"),
        name=name,
    )(x, w)


def _matmul_residual(x, w, res, tm, tn, name):
    m, k = x.shape
    n = w.shape[1]
    return pl.pallas_call(
        _mm_res_kernel,
        grid=(m // tm, n // tn),
        in_specs=[pl.BlockSpec((tm, k), lambda i, j: (i, 0)),
                  pl.BlockSpec((k, tn), lambda i, j: (0, j)),
                  pl.BlockSpec((tm, tn), lambda i, j: (i, j))],
        out_specs=pl.BlockSpec((tm, tn), lambda i, j: (i, j)),
        out_shape=jax.ShapeDtypeStruct((m, n), F32),
        compiler_params=_params("parallel", "parallel"),
        name=name,
    )(x, w, res)


def _merge_kernel(xn_ref, ya_ref, yb_ref, yc_ref, yd_ref, wg_ref, bg_ref, wb_ref, o_ref):
    xn = xn_ref[...]
    acc = None
    for g, y_ref in enumerate((ya_ref, yb_ref, yc_ref, yd_ref)):
        gate = jax.nn.sigmoid(_dot(xn, wg_ref[g]) + bg_ref[g:g + 1, :])
        term = gate * _dot(y_ref[...], wb_ref[g])
        acc = term if acc is None else acc + term
    o_ref[...] = acc.astype(o_ref.dtype)


def _merge(xn, ys, wg, bg, wb, tm, tn):
    s, d = xn.shape
    w = ys[0].shape[1]
    y_spec = pl.BlockSpec((tm, w), lambda i, j: (i, 0))
    return pl.pallas_call(
        _merge_kernel,
        grid=(s // tm, d // tn),
        in_specs=[pl.BlockSpec((tm, d), lambda i, j: (i, 0)),
                  y_spec, y_spec, y_spec, y_spec,
                  pl.BlockSpec((N_BRANCH, d, tn), lambda i, j: (0, 0, j)),
                  pl.BlockSpec((N_BRANCH, tn), lambda i, j: (0, j)),
                  pl.BlockSpec((N_BRANCH, w, tn), lambda i, j: (0, 0, j))],
        out_specs=pl.BlockSpec((tm, tn), lambda i, j: (i, j)),
        out_shape=jax.ShapeDtypeStruct((s, d), BF16),
        compiler_params=_params("parallel", "parallel"),
        name="merge",
    )(xn, *ys, wg, bg, wb)


def _ffn_up_kernel(x_ref, wg_ref, wu_ref, o_ref):
    x = x_ref[...]
    o_ref[...] = (jax.nn.silu(_dot(x, wg_ref[...])) * _dot(x, wu_ref[...])).astype(o_ref.dtype)


def _ffn_up(hn, wg, wu, tm, tn):
    s, d = hn.shape
    f = wg.shape[1]
    return pl.pallas_call(
        _ffn_up_kernel,
        grid=(s // tm, f // tn),
        in_specs=[pl.BlockSpec((tm, d), lambda i, j: (i, 0)),
                  pl.BlockSpec((d, tn), lambda i, j: (0, j)),
                  pl.BlockSpec((d, tn), lambda i, j: (0, j))],
        out_specs=pl.BlockSpec((tm, tn), lambda i, j: (i, j)),
        out_shape=jax.ShapeDtypeStruct((s, f), BF16),
        compiler_params=_params("parallel", "parallel"),
        name="ffn_up",
    )(hn, wg, wu)


def _col_spec(t, col):
    return pl.BlockSpec((t, BRANCH_W), lambda i: (i, col))


def _halo_spec(t, col):
    return pl.BlockSpec((8, BRANCH_W), lambda i: (jnp.maximum(i * (t // 8) - 1, 0), col))


def _full_spec(shape):
    nd = len(shape)
    return pl.BlockSpec(shape, lambda i: (0,) * nd)


def _shift_rows(x, prev8, j):
    xr = pltpu.roll(x, j, 0)
    pr = pltpu.roll(prev8, j, 0)
    rid = lax.broadcasted_iota(jnp.int32, pr.shape, 0)
    top = jnp.where(rid < j, pr, xr[:8])
    return jnp.concatenate([top, xr[8:]], axis=0)


def _causal_conv(x, prev8, w_ref, b_ref):
    acc = None
    for k in range(CONV_W):
        j = CONV_W - 1 - k
        xs = x if j == 0 else _shift_rows(x, prev8, j)
        term = xs * w_ref[k:k + 1, :]
        acc = term if acc is None else acc + term
    return acc + b_ref[...]


def _gmlp_kernel(u_ref, v_ref, gv_ref, ws_ref, bst_ref, o_ref):
    t = u_ref.shape[0]
    u = jax.nn.gelu(u_ref[...])
    v = (_rms(jax.nn.gelu(v_ref[...])) * gv_ref[...]).astype(BF16)
    row = lax.broadcasted_iota(jnp.int32, (CHUNK, CHUNK), 0)
    col = lax.broadcasted_iota(jnp.int32, (CHUNK, CHUNK), 1)
    for g in range(N_GROUPS):
        gs = slice(g * GROUP_DIM, (g + 1) * GROUP_DIM)
        ws = jnp.where(col <= row, ws_ref[g], 0.0).astype(BF16)
        bcol = bst_ref[:, g:g + 1]
        for c in range(t // CHUNK):
            cs = slice(c * CHUNK, (c + 1) * CHUNK)
            sp = _dot(ws, v[cs, gs]) + bcol
            o_ref[cs, gs] = (u[cs, gs] * sp).astype(o_ref.dtype)


def _gmlp(proj, gv, ws, bst, t):
    s = proj.shape[0]
    return pl.pallas_call(
        _gmlp_kernel,
        grid=(s // t,),
        in_specs=[_col_spec(t, COL_A_U), _col_spec(t, COL_A_V),
                  _full_spec((1, BRANCH_W)), _full_spec((N_GROUPS, CHUNK, CHUNK)),
                  _full_spec((CHUNK, N_GROUPS))],
        out_specs=pl.BlockSpec((t, BRANCH_W), lambda i: (i, 0)),
        out_shape=jax.ShapeDtypeStruct((s, BRANCH_W), BF16),
        compiler_params=_params("parallel"),
        name="gmlp",
    )(proj, proj, gv, ws, bst)


def _lin_scan(a, b):
    t = a.shape[0]
    rid = lax.broadcasted_iota(jnp.int32, a.shape, 0)
    d = 1
    while d < t:
        keep = rid >= d
        a_sh = jnp.where(keep, pltpu.roll(a, d, 0), 1.0)
        b_sh = jnp.where(keep, pltpu.roll(b, d, 0), 0.0)
        b = a * b_sh + b
        a = a * a_sh
        d *= 2
    return a, b


def _rglru_kernel(x_ref, xp_ref, gt_ref, cw_ref, cb_ref, wr_ref, br_ref, wi_ref, bi_ref, lam_ref,
                  o_ref, h_s):
    i = pl.program_id(0)

    @pl.when(i == 0)
    def _():
        h_s[...] = jnp.zeros_like(h_s)

    prev = jnp.where(i == 0, 0.0, xp_ref[...])
    xc = _causal_conv(x_ref[...], prev, cw_ref, cb_ref)
    xcb = xc.astype(BF16)
    r_parts, i_parts = [], []
    for g in range(N_GROUPS):
        gs = slice(g * GROUP_DIM, (g + 1) * GROUP_DIM)
        r_parts.append(_dot(xcb[:, gs], wr_ref[g]))
        i_parts.append(_dot(xcb[:, gs], wi_ref[g]))
    r = jax.nn.sigmoid(jnp.concatenate(r_parts, axis=1) + br_ref[...])
    ig = jax.nn.sigmoid(jnp.concatenate(i_parts, axis=1) + bi_ref[...])
    log_a = LRU_C * r * _log_sigmoid(lam_ref[...])
    a = jnp.exp(log_a)
    bx = jnp.sqrt(-jnp.tanh(log_a) * (a * a + 1.0)) * (ig * xc)
    a_cum, h_loc = _lin_scan(a, bx)
    h = h_loc + a_cum * h_s[0:1, :]
    t = h.shape[0]
    h_s[...] = jnp.broadcast_to(h[t - 1:t, :], h_s.shape)
    o_ref[...] = (h * jax.nn.gelu(gt_ref[...])).astype(o_ref.dtype)


def _rglru(proj, cw, cb, wr, br, wi, bi, lam, t):
    s = proj.shape[0]
    return pl.pallas_call(
        _rglru_kernel,
        grid=(s // t,),
        in_specs=[_col_spec(t, COL_B_X), _halo_spec(t, COL_B_X), _col_spec(t, COL_B_G),
                  _full_spec((CONV_W, BRANCH_W)), _full_spec((1, BRANCH_W)),
                  _full_spec((N_GROUPS, GROUP_DIM, GROUP_DIM)), _full_spec((1, BRANCH_W)),
                  _full_spec((N_GROUPS, GROUP_DIM, GROUP_DIM)), _full_spec((1, BRANCH_W)),
                  _full_spec((1, BRANCH_W))],
        out_specs=pl.BlockSpec((t, BRANCH_W), lambda i: (i, 0)),
        out_shape=jax.ShapeDtypeStruct((s, BRANCH_W), BF16),
        scratch_shapes=[pltpu.VMEM((8, BRANCH_W), F32)],
        compiler_params=_params("arbitrary"),
        name="rglru",
    )(proj, proj, proj, cw, cb, wr, br, wi, bi, lam)


def _mlstm_kernel(q_ref, qp_ref, k_ref, kp_ref, v_ref, og_ref, ifc_ref, ifr_ref,
                  cwq_ref, cbq_ref, cwk_ref, cbk_ref, bifc_ref, bifr_ref, gh_ref,
                  y_ref, ct_s, n_s, m_s):
    i = pl.program_id(0)

    @pl.when(i == 0)
    def _():
        ct_s[...] = jnp.zeros_like(ct_s)
        n_s[...] = jnp.zeros_like(n_s)
        m_s[...] = jnp.zeros_like(m_s)

    first = i == 0
    q = jax.nn.silu(_causal_conv(q_ref[...], jnp.where(first, 0.0, qp_ref[...]), cwq_ref, cbq_ref))
    k = jax.nn.silu(_causal_conv(k_ref[...], jnp.where(first, 0.0, kp_ref[...]), cwk_ref, cbk_ref))
    k = k * (GROUP_DIM ** -0.5)
    v = v_ref[...]
    og = og_ref[...]

    ifc = ifc_ref[...] + bifc_ref[...]
    ifr = ifr_ref[...] + bifr_ref[...]
    row = lax.broadcasted_iota(jnp.int32, (CHUNK, CHUNK), 0)
    col = lax.broadcasted_iota(jnp.int32, (CHUNK, CHUNK), 1)
    tril = col <= row
    tril_b = jnp.where(tril, 1.0, 0.0).astype(BF16)
    triu_b = jnp.where(row <= col, 1.0, 0.0).astype(BF16)
    bcc = sum(_dot(tril_b, piece) for piece in _split3(_log_sigmoid(ifc)))
    bcr = sum(_dot(piece, triu_b) for piece in _split3(_log_sigmoid(ifr)))

    for h in range(N_GROUPS):
        hs = slice(h * GROUP_DIM, (h + 1) * GROUP_DIM)
        qh, kh, vh = q[:, hs], k[:, hs], v[:, hs]
        ic_row, ic_col = ifr[h:h + 1, :], ifc[:, h:h + 1]
        bc_row, bc_col = bcr[N_GROUPS + h:N_GROUPS + h + 1, :], bcc[:, N_GROUPS + h:N_GROUPS + h + 1]
        m_st = m_s[h:h + 1, 0:1]
        n_row = n_s[h:h + 1, :]
        ct = ct_s[h]

        dmat = jnp.where(tril, bc_col - bc_row + ic_row, -jnp.inf)
        m_inter = bc_col + m_st
        m_t = jnp.maximum(m_inter, jnp.max(dmat, axis=-1, keepdims=True))
        kt = kh.T
        qb, vb = qh.astype(BF16), vh.astype(BF16)
        p = _dot(qb, kt.astype(BF16)) * jnp.exp(dmat - m_t)
        sc = jnp.exp(m_inter - m_t)
        num = _dot(p.astype(BF16), vb) + sc * _dot(qb, ct.astype(BF16))
        den = jnp.sum(p, axis=-1, keepdims=True) + sc * jnp.sum(qh * n_row, axis=-1, keepdims=True)
        hh = num / jnp.maximum(jnp.abs(den), jnp.exp(-m_t))

        b_last = bc_row[:, CHUNK - 1:CHUNK]
        g_row = b_last - bc_row + ic_row
        g_col = b_last - bc_col + ic_col
        m_new = jnp.maximum(b_last + m_st, jnp.max(g_row, axis=-1, keepdims=True))
        w_row = jnp.exp(g_row - m_new)
        w_col = jnp.exp(g_col - m_new)
        decay = jnp.exp(b_last + m_st - m_new)
        ct_s[h] = decay * ct + _dot((kt * w_row).astype(BF16), vb)
        n_s[h:h + 1, :] = decay * n_row + jnp.sum(kh * w_col, axis=0, keepdims=True)
        m_s[h:h + 1, :] = jnp.broadcast_to(m_new, (1, GROUP_DIM))

        hn = _rms(hh) * gh_ref[:, hs]
        y_ref[:, hs] = (jax.nn.sigmoid(og[:, hs]) * hn).astype(y_ref.dtype)


def _mlstm(proj, ifc, ifr, cwq, cbq, cwk, cbk, bifc, bifr, gh):
    s = proj.shape[0]
    t = CHUNK
    return pl.pallas_call(
        _mlstm_kernel,
        grid=(s // t,),
        in_specs=[_col_spec(t, COL_C_Q), _halo_spec(t, COL_C_Q),
                  _col_spec(t, COL_C_K), _halo_spec(t, COL_C_K),
                  _col_spec(t, COL_C_V), _col_spec(t, COL_C_O),
                  pl.BlockSpec((t, IF_PAD), lambda i: (i, 0)),
                  pl.BlockSpec((IF_ROWS, t), lambda i: (0, i)),
                  _full_spec((CONV_W, BRANCH_W)), _full_spec((1, BRANCH_W)),
                  _full_spec((CONV_W, BRANCH_W)), _full_spec((1, BRANCH_W)),
                  _full_spec((1, IF_PAD)), _full_spec((IF_ROWS, 1)),
                  _full_spec((1, BRANCH_W))],
        out_specs=pl.BlockSpec((t, BRANCH_W), lambda i: (i, 0)),
        out_shape=jax.ShapeDtypeStruct((s, BRANCH_W), BF16),
        scratch_shapes=[pltpu.VMEM((N_GROUPS, GROUP_DIM, GROUP_DIM), F32),
                        pltpu.VMEM((8, GROUP_DIM), F32),
                        pltpu.VMEM((8, GROUP_DIM), F32)],
        compiler_params=_params("arbitrary"),
        name="mlstm",
    )(proj, proj, proj, proj, proj, proj, ifc, ifr, cwq, cbq, cwk, cbk, bifc, bifr, gh)


def _sb_prep_kernel(q_ref, k_ref, v_ref, gq_ref, gk_ref, qo_ref, ko_ref, vo_ref):
    q, k, v = q_ref[...], k_ref[...], v_ref[...]
    for h in range(N_GROUPS):
        hs = slice(h * GROUP_DIM, (h + 1) * GROUP_DIM)
        qo_ref[h] = (_rms(q[:, hs]) * gq_ref[...] * (GROUP_DIM ** -0.5)).astype(BF16)
        ko_ref[h] = (_rms(k[:, hs]) * gk_ref[...]).astype(BF16)
        vo_ref[h] = v[:, hs].astype(BF16)


def _sb_prep(proj, gq, gk, t):
    s = proj.shape[0]
    hm_spec = pl.BlockSpec((N_GROUPS, t, GROUP_DIM), lambda i: (0, i, 0))
    hm_shape = jax.ShapeDtypeStruct((N_GROUPS, s, GROUP_DIM), BF16)
    return pl.pallas_call(
        _sb_prep_kernel,
        grid=(s // t,),
        in_specs=[_col_spec(t, COL_D_Q), _col_spec(t, COL_D_K), _col_spec(t, COL_D_V),
                  _full_spec((1, GROUP_DIM)), _full_spec((1, GROUP_DIM))],
        out_specs=[hm_spec, hm_spec, hm_spec],
        out_shape=[hm_shape, hm_shape, hm_shape],
        compiler_params=_params("parallel"),
        name="sb_prep",
    )(proj, proj, proj, gq, gk)


def _sb_kernel(q_ref, k_ref, v_ref, o_ref, *, tq):
    qi = pl.program_id(1)
    q = q_ref[0]
    row = lax.broadcasted_iota(jnp.int32, (tq, tq), 0)
    col = lax.broadcasted_iota(jnp.int32, (tq, tq), 1)
    later = jnp.where(row > col, 1.0, 0.0).astype(BF16)
    strict = col < row

    def block(start, masked, acc, run):
        kb = k_ref[0, pl.ds(start, tq), :]
        vb = v_ref[0, pl.ds(start, tq), :]
        z = _dot_nt(q, kb)
        sp = jnp.maximum(z, 0.0) + jnp.log1p(jnp.exp(-jnp.abs(z)))
        ln1m = -sp
        if masked:
            ln1m = jnp.where(strict, ln1m, 0.0)
        hi = ln1m.astype(BF16)
        lo = (ln1m - hi.astype(F32)).astype(BF16)
        rem = _dot(hi, later) + _dot(lo, later) + run
        att = jnp.exp(z - sp + rem)
        if masked:
            att = jnp.where(strict, att, 0.0)
        acc = acc + _dot(att.astype(BF16), vb)
        run = run + jnp.sum(ln1m, axis=-1, keepdims=True)
        return acc, run

    acc = jnp.zeros((tq, GROUP_DIM), F32)
    run = jnp.zeros((tq, 1), F32)
    acc, run = block(pl.multiple_of(qi * tq, tq), True, acc, run)

    def body(j, carry):
        return block(pl.multiple_of((qi - 1 - j) * tq, tq), False, *carry)

    acc, run = lax.fori_loop(0, qi, body, (acc, run))
    o_ref[...] = acc.astype(o_ref.dtype)


def _sb_attention(qh, kh, vh, tq):
    _, s, _ = qh.shape
    return pl.pallas_call(
        functools.partial(_sb_kernel, tq=tq),
        grid=(N_GROUPS, s // tq),
        in_specs=[pl.BlockSpec((1, tq, GROUP_DIM), lambda h, i: (h, i, 0)),
                  pl.BlockSpec((1, s, GROUP_DIM), lambda h, i: (h, 0, 0)),
                  pl.BlockSpec((1, s, GROUP_DIM), lambda h, i: (h, 0, 0))],
        out_specs=pl.BlockSpec((tq, GROUP_DIM), lambda h, i: (i, h)),
        out_shape=jax.ShapeDtypeStruct((s, BRANCH_W), BF16),
        compiler_params=_params("parallel", "parallel"),
        name="sb_attn",
    )(qh, kh, vh)


def _tile(s, pref):
    return min(s, pref)


def _layer(x, p):
    s = x.shape[0]
    xn, ifc, ifr = _norm_if(x, p["norm_mix"], p["w_if"], p["w_if_t"], _tile(s, 512))
    proj = _matmul(xn, p["w_in"], F32, _tile(s, 1024), 512, "in_proj")

    y_a = _gmlp(proj, p["gm_norm_v"], p["gm_w_s"], p["gm_b_s_t"], _tile(s, 512))
    y_b = _rglru(proj, p["lru_conv_w"], p["lru_conv_b"], p["lru_w_r"], p["lru_b_r"],
                 p["lru_w_i"], p["lru_b_i"], p["lru_lambda"], _tile(s, 512))
    y_c = _mlstm(proj, ifc, ifr, p["ml_conv_w_q"], p["ml_conv_b_q"], p["ml_conv_w_k"], p["ml_conv_b_k"],
                 p["ml_b_if_c"], p["ml_b_if_r"], p["ml_norm_h"])
    qh, kh, vh = _sb_prep(proj, p["sb_norm_q"], p["sb_norm_k"], _tile(s, 512))
    y_d = _sb_attention(qh, kh, vh, _tile(s, 256))

    merged = _merge(xn, (y_a, y_b, y_c, y_d), p["w_gate"], p["b_gate"], p["w_branch"], _tile(s, 1024), 256)
    x = _matmul_residual(merged, p["w_out"], x, _tile(s, 1024), 512, "out_proj")
    hn = _norm(x, p["norm_ffn"], _tile(s, 512))
    hid = _ffn_up(hn, p["w_ffn_gate"], p["w_ffn_up"], _tile(s, 1024), 512)
    return _matmul_residual(hid, p["w_ffn_down"], x, _tile(s, 512), 512, "ffn_down")


def _prepare_layer(l, norm_mix, w_in, gm_norm_v, gm_w_s, gm_b_s, lru_conv_w, lru_conv_b, lru_w_r, lru_b_r,
                   lru_w_i, lru_b_i, lru_lambda, ml_conv_w, ml_conv_b, ml_b_i, ml_b_f, ml_norm_h, sb_norm_q,
                   sb_norm_k, w_branch, w_gate, b_gate, w_out, norm_ffn, w_ffn_gate, w_ffn_up, w_ffn_down):
    w = BRANCH_W
    n_main = 8 * w
    w_l = w_in[l]
    w_main = jnp.concatenate([w_l[:, :n_main], w_l[:, n_main + 2 * N_GROUPS:]], axis=1).astype(BF16)
    w_if = w_l[:, n_main:n_main + 2 * N_GROUPS].astype(BF16)
    w_if_c = jnp.pad(w_if, ((0, 0), (0, IF_PAD - 2 * N_GROUPS)))
    w_if_r = jnp.pad(w_if.T, ((0, IF_ROWS - 2 * N_GROUPS), (0, 0)))
    b_if = jnp.concatenate([ml_b_i[l], ml_b_f[l]])
    return {
        "norm_mix": norm_mix[l][None, :],
        "w_in": w_main, "w_if": w_if_c, "w_if_t": w_if_r,
        "gm_norm_v": gm_norm_v[l][None, :],
        "gm_w_s": gm_w_s[l],
        "gm_b_s_t": gm_b_s[l].T,
        "lru_conv_w": lru_conv_w[l], "lru_conv_b": lru_conv_b[l][None, :],
        "lru_w_r": lru_w_r[l].astype(BF16), "lru_b_r": lru_b_r[l][None, :],
        "lru_w_i": lru_w_i[l].astype(BF16), "lru_b_i": lru_b_i[l][None, :],
        "lru_lambda": lru_lambda[l][None, :],
        "ml_conv_w_q": ml_conv_w[l][:, :w], "ml_conv_b_q": ml_conv_b[l][None, :w],
        "ml_conv_w_k": ml_conv_w[l][:, w:], "ml_conv_b_k": ml_conv_b[l][None, w:],
        "ml_b_if_c": jnp.pad(b_if, (0, IF_PAD - 2 * N_GROUPS))[None, :],
        "ml_b_if_r": jnp.pad(b_if, (0, IF_ROWS - 2 * N_GROUPS))[:, None],
        "ml_norm_h": ml_norm_h[l].reshape(1, w),
        "sb_norm_q": sb_norm_q[l][None, :], "sb_norm_k": sb_norm_k[l][None, :],
        "w_branch": w_branch[l].astype(BF16), "w_gate": w_gate[l].astype(BF16), "b_gate": b_gate[l],
        "w_out": w_out[l].astype(BF16),
        "norm_ffn": norm_ffn[l][None, :],
        "w_ffn_gate": w_ffn_gate[l].astype(BF16), "w_ffn_up": w_ffn_up[l].astype(BF16),
        "w_ffn_down": w_ffn_down[l].astype(BF16),
    }


def kernel(x, norm_mix, w_in, gm_norm_v, gm_w_s, gm_b_s, lru_conv_w, lru_conv_b, lru_w_r, lru_b_r, lru_w_i, lru_b_i, lru_lambda, ml_conv_w, ml_conv_b, ml_b_i, ml_b_f, ml_norm_h, sb_norm_q, sb_norm_k, w_branch, w_gate, b_gate, w_out, norm_ffn, w_ffn_gate, w_ffn_up, w_ffn_down):
    b, s, d = x.shape
    outs = []
    for bi in range(b):
        xb = x[bi]
        for l in range(norm_mix.shape[0]):
            p = _prepare_layer(l, norm_mix, w_in, gm_norm_v, gm_w_s, gm_b_s, lru_conv_w, lru_conv_b, lru_w_r,
                               lru_b_r, lru_w_i, lru_b_i, lru_lambda, ml_conv_w, ml_conv_b, ml_b_i, ml_b_f,
                               ml_norm_h, sb_norm_q, sb_norm_k, w_branch, w_gate, b_gate, w_out, norm_ffn,
                               w_ffn_gate, w_ffn_up, w_ffn_down)
            xb = _layer(xb, p)
        outs.append(xb)
    return jnp.stack(outs, axis=0)
```

```python
import functools

import jax
import jax.numpy as jnp
from jax import lax
from jax.experimental import pallas as pl
from jax.experimental.pallas import tpu as pltpu

D_MODEL = 2048
N_BRANCH = 4
BRANCH_W = D_MODEL // N_BRANCH
GROUP_DIM = 128
N_GROUPS = BRANCH_W // GROUP_DIM
CHUNK = 128
CONV_W = 4
LRU_C = 8.0
EPS = 1e-6
IF_PAD = 128
IF_ROWS = 16
LOG2E = 1.4426950408889634

F32 = jnp.float32
BF16 = jnp.bfloat16

VMEM_LIMIT_BYTES = 48 * 1024 * 1024

(COL_A_U, COL_A_V, COL_B_X, COL_B_G, COL_C_Q, COL_C_K, COL_C_V, COL_C_O,
 COL_D_Q, COL_D_K, COL_D_V) = range(11)


def _params(*sem):
    return pltpu.CompilerParams(dimension_semantics=sem, vmem_limit_bytes=VMEM_LIMIT_BYTES)


def _dot(a, b):
    return jnp.dot(a, b, preferred_element_type=F32)


def _dot_nt(a, b):
    return lax.dot_general(a, b, (((1,), (1,)), ((), ())), preferred_element_type=F32)


def _log_sigmoid(x):
    return jnp.minimum(x, 0.0) - jnp.log1p(jnp.exp(-jnp.abs(x)))


def _rms(x):
    return x * lax.rsqrt(jnp.mean(x * x, axis=-1, keepdims=True) + EPS)


def _split3(a):
    a1 = a.astype(BF16)
    r1 = a - a1.astype(F32)
    a2 = r1.astype(BF16)
    a3 = (r1 - a2.astype(F32)).astype(BF16)
    return a1, a2, a3


def _norm_if_kernel(x_ref, g_ref, wif_ref, wift_ref, xn_ref, ifc_ref, ifr_ref):
    xn = (_rms(x_ref[...]) * g_ref[...]).astype(BF16)
    xn_ref[...] = xn
    ifc_ref[...] = _dot(xn, wif_ref[...])
    ifr_ref[...] = _dot_nt(wift_ref[...], xn)


def _norm_kernel(x_ref, g_ref, xn_ref):
    xn_ref[...] = (_rms(x_ref[...]) * g_ref[...]).astype(BF16)


def _norm_if(x, g, wif, wift, tm):
    s, d = x.shape
    return pl.pallas_call(
        _norm_if_kernel,
        grid=(s // tm,),
        in_specs=[pl.BlockSpec((tm, d), lambda i: (i, 0)),
                  pl.BlockSpec((1, d), lambda i: (0, 0)),
                  pl.BlockSpec((d, IF_PAD), lambda i: (0, 0)),
                  pl.BlockSpec((IF_ROWS, d), lambda i: (0, 0))],
        out_specs=[pl.BlockSpec((tm, d), lambda i: (i, 0)),
                   pl.BlockSpec((tm, IF_PAD), lambda i: (i, 0)),
                   pl.BlockSpec((IF_ROWS, tm), lambda i: (0, i))],
        out_shape=[jax.ShapeDtypeStruct((s, d), BF16),
                   jax.ShapeDtypeStruct((s, IF_PAD), F32),
                   jax.ShapeDtypeStruct((IF_ROWS, s), F32)],
        compiler_params=_params("parallel"),
        name="norm_if",
    )(x, g, wif, wift)


def _norm(x, g, tm):
    s, d = x.shape
    return pl.pallas_call(
        _norm_kernel,
        grid=(s // tm,),
        in_specs=[pl.BlockSpec((tm, d), lambda i: (i, 0)),
                  pl.BlockSpec((1, d), lambda i: (0, 0))],
        out_specs=pl.BlockSpec((tm, d), lambda i: (i, 0)),
        out_shape=jax.ShapeDtypeStruct((s, d), BF16),
        compiler_params=_params("parallel"),
        name="norm",
    )(x, g)


def _mm_kernel(x_ref, w_ref, o_ref):
    o_ref[...] = _dot(x_ref[...], w_ref[...]).astype(o_ref.dtype)


def _mm_res_kernel(x_ref, w_ref, r_ref, o_ref):
    o_ref[...] = r_ref[...] + _dot(x_ref[...], w_ref[...])


def _matmul(x, w, out_dtype, tm, tn, name):
    m, k = x.shape
    n = w.shape[1]
    return pl.pallas_call(
        _mm_kernel,
        grid=(m // tm, n // tn),
        in_specs=[pl.BlockSpec((tm, k), lambda i, j: (i, 0)),
                  pl.BlockSpec((k, tn), lambda i, j: (0, j))],
        out_specs=pl.BlockSpec((tm, tn), lambda i, j: (i, j)),
        out_shape=jax.ShapeDtypeStruct((m, n), out_dtype),
        compiler_params=_params("parallel", "parallel"),
        name=name,
    )(x, w)


def _matmul_residual(x, w, res, tm, tn, name):
    m, k = x.shape
    n = w.shape[1]
    return pl.pallas_call(
        _mm_res_kernel,
        grid=(m // tm, n // tn),
        in_specs=[pl.BlockSpec((tm, k), lambda i, j: (i, 0)),
                  pl.BlockSpec((k, tn), lambda i, j: (0, j)),
                  pl.BlockSpec((tm, tn), lambda i, j: (i, j))],
        out_specs=pl.BlockSpec((tm, tn), lambda i, j: (i, j)),
        out_shape=jax.ShapeDtypeStruct((m, n), F32),
        compiler_params=_params("parallel", "parallel"),
        name=name,
    )(x, w, res)


def _merge_kernel(xn_ref, ya_ref, yb_ref, yc_ref, yd_ref, wg_ref, bg_ref, wb_ref, o_ref):
    xn = xn_ref[...]
    acc = None
    for g, y_ref in enumerate((ya_ref, yb_ref, yc_ref, yd_ref)):
        gate = jax.nn.sigmoid(_dot(xn, wg_ref[g]) + bg_ref[g:g + 1, :])
        term = gate * _dot(y_ref[...], wb_ref[g])
        acc = term if acc is None else acc + term
    o_ref[...] = acc.astype(o_ref.dtype)


def _merge(xn, ys, wg, bg, wb, tm, tn):
    s, d = xn.shape
    w = ys[0].shape[1]
    y_spec = pl.BlockSpec((tm, w), lambda i, j: (i, 0))
    return pl.pallas_call(
        _merge_kernel,
        grid=(s // tm, d // tn),
        in_specs=[pl.BlockSpec((tm, d), lambda i, j: (i, 0)),
                  y_spec, y_spec, y_spec, y_spec,
                  pl.BlockSpec((N_BRANCH, d, tn), lambda i, j: (0, 0, j)),
                  pl.BlockSpec((N_BRANCH, tn), lambda i, j: (0, j)),
                  pl.BlockSpec((N_BRANCH, w, tn), lambda i, j: (0, 0, j))],
        out_specs=pl.BlockSpec((tm, tn), lambda i, j: (i, j)),
        out_shape=jax.ShapeDtypeStruct((s, d), BF16),
        compiler_params=_params("parallel", "parallel"),
        name="merge",
    )(xn, *ys, wg, bg, wb)


def _ffn_up_kernel(x_ref, wg_ref, wu_ref, o_ref):
    x = x_ref[...]
    o_ref[...] = (jax.nn.silu(_dot(x, wg_ref[...])) * _dot(x, wu_ref[...])).astype(o_ref.dtype)


def _ffn_up(hn, wg, wu, tm, tn):
    s, d = hn.shape
    f = wg.shape[1]
    return pl.pallas_call(
        _ffn_up_kernel,
        grid=(s // tm, f // tn),
        in_specs=[pl.BlockSpec((tm, d), lambda i, j: (i, 0)),
                  pl.BlockSpec((d, tn), lambda i, j: (0, j)),
                  pl.BlockSpec((d, tn), lambda i, j: (0, j))],
        out_specs=pl.BlockSpec((tm, tn), lambda i, j: (i, j)),
        out_shape=jax.ShapeDtypeStruct((s, f), BF16),
        compiler_params=_params("parallel", "parallel"),
        name="ffn_up",
    )(hn, wg, wu)


def _col_spec(t, col):
    return pl.BlockSpec((t, BRANCH_W), lambda i: (i, col))


def _halo_spec(t, col):
    return pl.BlockSpec((8, BRANCH_W), lambda i: (jnp.maximum(i * (t // 8) - 1, 0), col))


def _full_spec(shape):
    nd = len(shape)
    return pl.BlockSpec(shape, lambda i: (0,) * nd)


def _shift_rows(x, prev8, j):
    xr = pltpu.roll(x, j, 0)
    pr = pltpu.roll(prev8, j, 0)
    rid = lax.broadcasted_iota(jnp.int32, pr.shape, 0)
    top = jnp.where(rid < j, pr, xr[:8])
    return jnp.concatenate([top, xr[8:]], axis=0)


def _causal_conv(x, prev8, w_ref, b_ref):
    acc = None
    for k in range(CONV_W):
        j = CONV_W - 1 - k
        xs = x if j == 0 else _shift_rows(x, prev8, j)
        term = xs * w_ref[k:k + 1, :]
        acc = term if acc is None else acc + term
    return acc + b_ref[...]


def _gmlp_kernel(u_ref, v_ref, gv_ref, ws_ref, bst_ref, o_ref):
    t = u_ref.shape[0]
    u = jax.nn.gelu(u_ref[...])
    v = (_rms(jax.nn.gelu(v_ref[...])) * gv_ref[...]).astype(BF16)
    row = lax.broadcasted_iota(jnp.int32, (CHUNK, CHUNK), 0)
    col = lax.broadcasted_iota(jnp.int32, (CHUNK, CHUNK), 1)
    for g in range(N_GROUPS):
        gs = slice(g * GROUP_DIM, (g + 1) * GROUP_DIM)
        ws = jnp.where(col <= row, ws_ref[g], 0.0).astype(BF16)
        bcol = bst_ref[:, g:g + 1]
        for c in range(t // CHUNK):
            cs = slice(c * CHUNK, (c + 1) * CHUNK)
            sp = _dot(ws, v[cs, gs]) + bcol
            o_ref[cs, gs] = (u[cs, gs] * sp).astype(o_ref.dtype)


def _gmlp(proj, gv, ws, bst, t):
    s = proj.shape[0]
    return pl.pallas_call(
        _gmlp_kernel,
        grid=(s // t,),
        in_specs=[_col_spec(t, COL_A_U), _col_spec(t, COL_A_V),
                  _full_spec((1, BRANCH_W)), _full_spec((N_GROUPS, CHUNK, CHUNK)),
                  _full_spec((CHUNK, N_GROUPS))],
        out_specs=pl.BlockSpec((t, BRANCH_W), lambda i: (i, 0)),
        out_shape=jax.ShapeDtypeStruct((s, BRANCH_W), BF16),
        compiler_params=_params("parallel"),
        name="gmlp",
    )(proj, proj, gv, ws, bst)


def _lin_scan(a, b):
    t = a.shape[0]
    rid = lax.broadcasted_iota(jnp.int32, a.shape, 0)
    d = 1
    while d < t:
        keep = rid >= d
        a_sh = jnp.where(keep, pltpu.roll(a, d, 0), 1.0)
        b_sh = jnp.where(keep, pltpu.roll(b, d, 0), 0.0)
        b = a * b_sh + b
        a = a * a_sh
        d *= 2
    return a, b


def _rglru_kernel(x_ref, xp_ref, gt_ref, cw_ref, cb_ref, wr_ref, br_ref, wi_ref, bi_ref, lam_ref,
                  o_ref, h_s):
    i = pl.program_id(0)

    @pl.when(i == 0)
    def _():
        h_s[...] = jnp.zeros_like(h_s)

    prev = jnp.where(i == 0, 0.0, xp_ref[...])
    xc = _causal_conv(x_ref[...], prev, cw_ref, cb_ref)
    xcb = xc.astype(BF16)
    r_parts, i_parts = [], []
    for g in range(N_GROUPS):
        gs = slice(g * GROUP_DIM, (g + 1) * GROUP_DIM)
        r_parts.append(_dot(xcb[:, gs], wr_ref[g]))
        i_parts.append(_dot(xcb[:, gs], wi_ref[g]))
    r = jax.nn.sigmoid(jnp.concatenate(r_parts, axis=1) + br_ref[...])
    ig = jax.nn.sigmoid(jnp.concatenate(i_parts, axis=1) + bi_ref[...])
    log_a = LRU_C * r * _log_sigmoid(lam_ref[...])
    a = jnp.exp(log_a)
    bx = jnp.sqrt(-jnp.tanh(log_a) * (a * a + 1.0)) * (ig * xc)
    a_cum, h_loc = _lin_scan(a, bx)
    h = h_loc + a_cum * h_s[0:1, :]
    t = h.shape[0]
    h_s[...] = jnp.broadcast_to(h[t - 1:t, :], h_s.shape)
    o_ref[...] = (h * jax.nn.gelu(gt_ref[...])).astype(o_ref.dtype)


def _rglru(proj, cw, cb, wr, br, wi, bi, lam, t):
    s = proj.shape[0]
    return pl.pallas_call(
        _rglru_kernel,
        grid=(s // t,),
        in_specs=[_col_spec(t, COL_B_X), _halo_spec(t, COL_B_X), _col_spec(t, COL_B_G),
                  _full_spec((CONV_W, BRANCH_W)), _full_spec((1, BRANCH_W)),
                  _full_spec((N_GROUPS, GROUP_DIM, GROUP_DIM)), _full_spec((1, BRANCH_W)),
                  _full_spec((N_GROUPS, GROUP_DIM, GROUP_DIM)), _full_spec((1, BRANCH_W)),
                  _full_spec((1, BRANCH_W))],
        out_specs=pl.BlockSpec((t, BRANCH_W), lambda i: (i, 0)),
        out_shape=jax.ShapeDtypeStruct((s, BRANCH_W), BF16),
        scratch_shapes=[pltpu.VMEM((8, BRANCH_W), F32)],
        compiler_params=_params("arbitrary"),
        name="rglru",
    )(proj, proj, proj, cw, cb, wr, br, wi, bi, lam)


def _mlstm_kernel(q_ref, qp_ref, k_ref, kp_ref, v_ref, og_ref, ifc_ref, ifr_ref,
                  cwq_ref, cbq_ref, cwk_ref, cbk_ref, bifc_ref, bifr_ref, gh_ref,
                  y_ref, ct_s, n_s, m_s):
    i = pl.program_id(0)

    @pl.when(i == 0)
    def _():
        ct_s[...] = jnp.zeros_like(ct_s)
        n_s[...] = jnp.zeros_like(n_s)
        m_s[...] = jnp.zeros_like(m_s)

    first = i == 0
    q = jax.nn.silu(_causal_conv(q_ref[...], jnp.where(first, 0.0, qp_ref[...]), cwq_ref, cbq_ref))
    k = jax.nn.silu(_causal_conv(k_ref[...], jnp.where(first, 0.0, kp_ref[...]), cwk_ref, cbk_ref))
    k = k * (GROUP_DIM ** -0.5)
    v = v_ref[...]
    og = og_ref[...]

    ifc = ifc_ref[...] + bifc_ref[...]
    ifr = ifr_ref[...] + bifr_ref[...]
    row = lax.broadcasted_iota(jnp.int32, (CHUNK, CHUNK), 0)
    col = lax.broadcasted_iota(jnp.int32, (CHUNK, CHUNK), 1)
    tril = col <= row
    tril_b = jnp.where(tril, 1.0, 0.0).astype(BF16)
    triu_b = jnp.where(row <= col, 1.0, 0.0).astype(BF16)
    bcc = sum(_dot(tril_b, piece) for piece in _split3(_log_sigmoid(ifc)))
    bcr = sum(_dot(piece, triu_b) for piece in _split3(_log_sigmoid(ifr)))

    for h in range(N_GROUPS):
        hs = slice(h * GROUP_DIM, (h + 1) * GROUP_DIM)
        qh, kh, vh = q[:, hs], k[:, hs], v[:, hs]
        ic_row, ic_col = ifr[h:h + 1, :], ifc[:, h:h + 1]
        bc_row, bc_col = bcr[N_GROUPS + h:N_GROUPS + h + 1, :], bcc[:, N_GROUPS + h:N_GROUPS + h + 1]
        m_st = m_s[h:h + 1, 0:1]
        n_row = n_s[h:h + 1, :]
        ct = ct_s[h]

        dmat = jnp.where(tril, bc_col - bc_row + ic_row, -jnp.inf)
        m_inter = bc_col + m_st
        m_t = jnp.maximum(m_inter, jnp.max(dmat, axis=-1, keepdims=True))
        kt = kh.T
        qb, vb = qh.astype(BF16), vh.astype(BF16)
        p = _dot(qb, kt.astype(BF16)) * jnp.exp(dmat - m_t)
        sc = jnp.exp(m_inter - m_t)
        num = _dot(p.astype(BF16), vb) + sc * _dot(qb, ct.astype(BF16))
        den = jnp.sum(p, axis=-1, keepdims=True) + sc * jnp.sum(qh * n_row, axis=-1, keepdims=True)
        hh = num / jnp.maximum(jnp.abs(den), jnp.exp(-m_t))

        b_last = bc_row[:, CHUNK - 1:CHUNK]
        g_row = b_last - bc_row + ic_row
        g_col = b_last - bc_col + ic_col
        m_new = jnp.maximum(b_last + m_st, jnp.max(g_row, axis=-1, keepdims=True))
        w_row = jnp.exp(g_row - m_new)
        w_col = jnp.exp(g_col - m_new)
        decay = jnp.exp(b_last + m_st - m_new)
        ct_s[h] = decay * ct + _dot((kt * w_row).astype(BF16), vb)
        n_s[h:h + 1, :] = decay * n_row + jnp.sum(kh * w_col, axis=0, keepdims=True)
        m_s[h:h + 1, :] = jnp.broadcast_to(m_new, (1, GROUP_DIM))

        hn = _rms(hh) * gh_ref[:, hs]
        y_ref[:, hs] = (jax.nn.sigmoid(og[:, hs]) * hn).astype(y_ref.dtype)


def _mlstm(proj, ifc, ifr, cwq, cbq, cwk, cbk, bifc, bifr, gh):
    s = proj.shape[0]
    t = CHUNK
    return pl.pallas_call(
        _mlstm_kernel,
        grid=(s // t,),
        in_specs=[_col_spec(t, COL_C_Q), _halo_spec(t, COL_C_Q),
                  _col_spec(t, COL_C_K), _halo_spec(t, COL_C_K),
                  _col_spec(t, COL_C_V), _col_spec(t, COL_C_O),
                  pl.BlockSpec((t, IF_PAD), lambda i: (i, 0)),
                  pl.BlockSpec((IF_ROWS, t), lambda i: (0, i)),
                  _full_spec((CONV_W, BRANCH_W)), _full_spec((1, BRANCH_W)),
                  _full_spec((CONV_W, BRANCH_W)), _full_spec((1, BRANCH_W)),
                  _full_spec((1, IF_PAD)), _full_spec((IF_ROWS, 1)),
                  _full_spec((1, BRANCH_W))],
        out_specs=pl.BlockSpec((t, BRANCH_W), lambda i: (i, 0)),
        out_shape=jax.ShapeDtypeStruct((s, BRANCH_W), BF16),
        scratch_shapes=[pltpu.VMEM((N_GROUPS, GROUP_DIM, GROUP_DIM), F32),
                        pltpu.VMEM((8, GROUP_DIM), F32),
                        pltpu.VMEM((8, GROUP_DIM), F32)],
        compiler_params=_params("arbitrary"),
        name="mlstm",
    )(proj, proj, proj, proj, proj, proj, ifc, ifr, cwq, cbq, cwk, cbk, bifc, bifr, gh)


def _sb_prep_kernel(q_ref, k_ref, v_ref, gq_ref, gk_ref, qo_ref, ko_ref, vo_ref):
    q, k, v = q_ref[...], k_ref[...], v_ref[...]
    for h in range(N_GROUPS):
        hs = slice(h * GROUP_DIM, (h + 1) * GROUP_DIM)
        qo_ref[h] = (_rms(q[:, hs]) * gq_ref[...] * (GROUP_DIM ** -0.5 * LOG2E)).astype(BF16)
        ko_ref[h] = (_rms(k[:, hs]) * gk_ref[...]).astype(BF16)
        vo_ref[h] = v[:, hs].astype(BF16)


def _sb_prep(proj, gq, gk, t):
    s = proj.shape[0]
    hm_spec = pl.BlockSpec((N_GROUPS, t, GROUP_DIM), lambda i: (0, i, 0))
    hm_shape = jax.ShapeDtypeStruct((N_GROUPS, s, GROUP_DIM), BF16)
    return pl.pallas_call(
        _sb_prep_kernel,
        grid=(s // t,),
        in_specs=[_col_spec(t, COL_D_Q), _col_spec(t, COL_D_K), _col_spec(t, COL_D_V),
                  _full_spec((1, GROUP_DIM)), _full_spec((1, GROUP_DIM))],
        out_specs=[hm_spec, hm_spec, hm_spec],
        out_shape=[hm_shape, hm_shape, hm_shape],
        compiler_params=_params("parallel"),
        name="sb_prep",
    )(proj, proj, proj, gq, gk)


def _sb_kernel(q_ref, k_ref, v_ref, o_ref, *, tq, sub):
    qi = pl.program_id(1)
    nsub = tq // sub
    q = q_ref[0]
    krow = lax.broadcasted_iota(jnp.int32, (sub, sub), 0)
    kcol = lax.broadcasted_iota(jnp.int32, (sub, sub), 1)
    from_j = jnp.where(krow >= kcol, 1.0, 0.0).astype(BF16)
    from_j2 = jnp.concatenate([from_j, from_j], axis=0)

    def piece(qa, start, strict, run):
        kb = k_ref[0, pl.ds(start, sub), :]
        z = _dot_nt(qa, kb)
        neg_abs = lax.bitcast_convert_type(
            lax.bitcast_convert_type(z, jnp.uint32) | jnp.uint32(0x80000000), F32)
        sp = jnp.maximum(z, 0.0) + jnp.log(1.0 + jnp.exp2(neg_abs)) * LOG2E
        if strict is not None:
            sp = jnp.where(strict, sp, 0.0)
        hi = sp.astype(BF16)
        lo = (sp - hi.astype(F32)).astype(BF16)
        att = jnp.exp2(z - _dot(jnp.concatenate([hi, lo], axis=1), from_j2) - run)
        if strict is not None:
            att = jnp.where(strict, att, 0.0)
        return att.astype(BF16), run + jnp.sum(sp, axis=-1, keepdims=True)

    acc = jnp.zeros((tq, GROUP_DIM), F32)
    run = jnp.zeros((tq, 1), F32)
    for b in range(nsub - 1, -1, -1):
        r0 = b * sub
        qpos = lax.broadcasted_iota(jnp.int32, (tq - r0, sub), 0)
        kpos = lax.broadcasted_iota(jnp.int32, (tq - r0, sub), 1)
        start = pl.multiple_of(qi * tq + r0, sub)
        att, run_b = piece(q[r0:], start, kpos < qpos, run[r0:])
        acc_b = acc[r0:] + _dot(att, v_ref[0, pl.ds(start, sub), :])
        acc = acc_b if b == 0 else jnp.concatenate([acc[:r0], acc_b], axis=0)
        run = run_b if b == 0 else jnp.concatenate([run[:r0], run_b], axis=0)

    def tile(j, carry):
        acc, run = carry
        base = (qi - 1 - j) * tq
        atts = []
        for b in range(nsub - 1, -1, -1):
            att, run = piece(q, pl.multiple_of(base + b * sub, sub), None, run)
            atts.append(att)
        vb = v_ref[0, pl.ds(pl.multiple_of(base, tq), tq), :]
        return acc + _dot(jnp.concatenate(atts[::-1], axis=1), vb), run

    acc, run = lax.fori_loop(0, qi, tile, (acc, run))
    o_ref[...] = acc.astype(o_ref.dtype)


def _sb_attention(qh, kh, vh, tq):
    _, s, _ = qh.shape
    return pl.pallas_call(
        functools.partial(_sb_kernel, tq=tq, sub=min(tq, 256)),
        grid=(N_GROUPS, s // tq),
        in_specs=[pl.BlockSpec((1, tq, GROUP_DIM), lambda h, i: (h, i, 0)),
                  pl.BlockSpec((1, s, GROUP_DIM), lambda h, i: (h, 0, 0)),
                  pl.BlockSpec((1, s, GROUP_DIM), lambda h, i: (h, 0, 0))],
        out_specs=pl.BlockSpec((tq, GROUP_DIM), lambda h, i: (i, h)),
        out_shape=jax.ShapeDtypeStruct((s, BRANCH_W), BF16),
        compiler_params=_params("parallel", "parallel"),
        name="sb_attn",
    )(qh, kh, vh)


def _tile(s, pref):
    return min(s, pref)


def _layer(x, p):
    s = x.shape[0]
    xn, ifc, ifr = _norm_if(x, p["norm_mix"], p["w_if"], p["w_if_t"], _tile(s, 512))
    proj = _matmul(xn, p["w_in"], F32, _tile(s, 1024), 512, "in_proj")

    y_a = _gmlp(proj, p["gm_norm_v"], p["gm_w_s"], p["gm_b_s_t"], _tile(s, 512))
    y_b = _rglru(proj, p["lru_conv_w"], p["lru_conv_b"], p["lru_w_r"], p["lru_b_r"],
                 p["lru_w_i"], p["lru_b_i"], p["lru_lambda"], _tile(s, 512))
    y_c = _mlstm(proj, ifc, ifr, p["ml_conv_w_q"], p["ml_conv_b_q"], p["ml_conv_w_k"], p["ml_conv_b_k"],
                 p["ml_b_if_c"], p["ml_b_if_r"], p["ml_norm_h"])
    qh, kh, vh = _sb_prep(proj, p["sb_norm_q"], p["sb_norm_k"], _tile(s, 512))
    y_d = _sb_attention(qh, kh, vh, _tile(s, 1024))

    merged = _merge(xn, (y_a, y_b, y_c, y_d), p["w_gate"], p["b_gate"], p["w_branch"], _tile(s, 1024), 256)
    x = _matmul_residual(merged, p["w_out"], x, _tile(s, 1024), 512, "out_proj")
    hn = _norm(x, p["norm_ffn"], _tile(s, 512))
    hid = _ffn_up(hn, p["w_ffn_gate"], p["w_ffn_up"], _tile(s, 1024), 512)
    return _matmul_residual(hid, p["w_ffn_down"], x, _tile(s, 512), 512, "ffn_down")


def _prepare_layer(l, norm_mix, w_in, gm_norm_v, gm_w_s, gm_b_s, lru_conv_w, lru_conv_b, lru_w_r, lru_b_r,
                   lru_w_i, lru_b_i, lru_lambda, ml_conv_w, ml_conv_b, ml_b_i, ml_b_f, ml_norm_h, sb_norm_q,
                   sb_norm_k, w_branch, w_gate, b_gate, w_out, norm_ffn, w_ffn_gate, w_ffn_up, w_ffn_down):
    w = BRANCH_W
    n_main = 8 * w
    w_l = w_in[l]
    w_main = jnp.concatenate([w_l[:, :n_main], w_l[:, n_main + 2 * N_GROUPS:]], axis=1).astype(BF16)
    w_if = w_l[:, n_main:n_main + 2 * N_GROUPS].astype(BF16)
    w_if_c = jnp.pad(w_if, ((0, 0), (0, IF_PAD - 2 * N_GROUPS)))
    w_if_r = jnp.pad(w_if.T, ((0, IF_ROWS - 2 * N_GROUPS), (0, 0)))
    b_if = jnp.concatenate([ml_b_i[l], ml_b_f[l]])
    return {
        "norm_mix": norm_mix[l][None, :],
        "w_in": w_main, "w_if": w_if_c, "w_if_t": w_if_r,
        "gm_norm_v": gm_norm_v[l][None, :],
        "gm_w_s": gm_w_s[l],
        "gm_b_s_t": gm_b_s[l].T,
        "lru_conv_w": lru_conv_w[l], "lru_conv_b": lru_conv_b[l][None, :],
        "lru_w_r": lru_w_r[l].astype(BF16), "lru_b_r": lru_b_r[l][None, :],
        "lru_w_i": lru_w_i[l].astype(BF16), "lru_b_i": lru_b_i[l][None, :],
        "lru_lambda": lru_lambda[l][None, :],
        "ml_conv_w_q": ml_conv_w[l][:, :w], "ml_conv_b_q": ml_conv_b[l][None, :w],
        "ml_conv_w_k": ml_conv_w[l][:, w:], "ml_conv_b_k": ml_conv_b[l][None, w:],
        "ml_b_if_c": jnp.pad(b_if, (0, IF_PAD - 2 * N_GROUPS))[None, :],
        "ml_b_if_r": jnp.pad(b_if, (0, IF_ROWS - 2 * N_GROUPS))[:, None],
        "ml_norm_h": ml_norm_h[l].reshape(1, w),
        "sb_norm_q": sb_norm_q[l][None, :], "sb_norm_k": sb_norm_k[l][None, :],
        "w_branch": w_branch[l].astype(BF16), "w_gate": w_gate[l].astype(BF16), "b_gate": b_gate[l],
        "w_out": w_out[l].astype(BF16),
        "norm_ffn": norm_ffn[l][None, :],
        "w_ffn_gate": w_ffn_gate[l].astype(BF16), "w_ffn_up": w_ffn_up[l].astype(BF16),
        "w_ffn_down": w_ffn_down[l].astype(BF16),
    }


def kernel(x, norm_mix, w_in, gm_norm_v, gm_w_s, gm_b_s, lru_conv_w, lru_conv_b, lru_w_r, lru_b_r, lru_w_i, lru_b_i, lru_lambda, ml_conv_w, ml_conv_b, ml_b_i, ml_b_f, ml_norm_h, sb_norm_q, sb_norm_k, w_branch, w_gate, b_gate, w_out, norm_ffn, w_ffn_gate, w_ffn_up, w_ffn_down):
    b, s, d = x.shape
    outs = []
    for bi in range(b):
        xb = x[bi]
        for l in range(norm_mix.shape[0]):
            p = _prepare_layer(l, norm_mix, w_in, gm_norm_v, gm_w_s, gm_b_s, lru_conv_w, lru_conv_b, lru_w_r,
                               lru_b_r, lru_w_i, lru_b_i, lru_lambda, ml_conv_w, ml_conv_b, ml_b_i, ml_b_f,
                               ml_norm_h, sb_norm_q, sb_norm_k, w_branch, w_gate, b_gate, w_out, norm_ffn,
                               w_ffn_gate, w_ffn_up, w_ffn_down)
            xb = _layer(xb, p)
        outs.append(xb)
    return jnp.stack(outs, axis=0)
```

```python
import functools

import jax
import jax.numpy as jnp
from jax import lax
from jax.experimental import pallas as pl
from jax.experimental.pallas import tpu as pltpu

D_MODEL = 2048
N_BRANCH = 4
BRANCH_W = D_MODEL // N_BRANCH
GROUP_DIM = 128
N_GROUPS = BRANCH_W // GROUP_DIM
CHUNK = 128
CONV_W = 4
LRU_C = 8.0
EPS = 1e-6
IF_PAD = 128
IF_ROWS = 16
LOG2E = 1.4426950408889634

F32 = jnp.float32
BF16 = jnp.bfloat16

VMEM_LIMIT_BYTES = 48 * 1024 * 1024

COL_A_U, COL_A_V, COL_B_X, COL_B_G, COL_C_Q, COL_C_K, COL_C_V, COL_C_O = range(8)
COL_D_Q, COL_D_K, COL_D_V = range(3)


def _params(*sem):
    return pltpu.CompilerParams(dimension_semantics=sem, vmem_limit_bytes=VMEM_LIMIT_BYTES)


def _dot(a, b):
    return jnp.dot(a, b, preferred_element_type=F32)


def _dot_nt(a, b):
    return lax.dot_general(a, b, (((1,), (1,)), ((), ())), preferred_element_type=F32)


def _log_sigmoid(x):
    return jnp.minimum(x, 0.0) - jnp.log1p(jnp.exp(-jnp.abs(x)))


def _rms(x):
    return x * lax.rsqrt(jnp.mean(x * x, axis=-1, keepdims=True) + EPS)


def _split3(a):
    a1 = a.astype(BF16)
    r1 = a - a1.astype(F32)
    a2 = r1.astype(BF16)
    a3 = (r1 - a2.astype(F32)).astype(BF16)
    return a1, a2, a3


def _norm_if_kernel(x_ref, g_ref, wif_ref, wift_ref, xn_ref, ifc_ref, ifr_ref):
    xn = (_rms(x_ref[...]) * g_ref[...]).astype(BF16)
    xn_ref[...] = xn
    ifc_ref[...] = _dot(xn, wif_ref[...])
    ifr_ref[...] = _dot_nt(wift_ref[...], xn)


def _norm_kernel(x_ref, g_ref, xn_ref):
    xn_ref[...] = (_rms(x_ref[...]) * g_ref[...]).astype(BF16)


def _norm_if(x, g, wif, wift, tm):
    s, d = x.shape
    return pl.pallas_call(
        _norm_if_kernel,
        grid=(s // tm,),
        in_specs=[pl.BlockSpec((tm, d), lambda i: (i, 0)),
                  pl.BlockSpec((1, d), lambda i: (0, 0)),
                  pl.BlockSpec((d, IF_PAD), lambda i: (0, 0)),
                  pl.BlockSpec((IF_ROWS, d), lambda i: (0, 0))],
        out_specs=[pl.BlockSpec((tm, d), lambda i: (i, 0)),
                   pl.BlockSpec((tm, IF_PAD), lambda i: (i, 0)),
                   pl.BlockSpec((IF_ROWS, tm), lambda i: (0, i))],
        out_shape=[jax.ShapeDtypeStruct((s, d), BF16),
                   jax.ShapeDtypeStruct((s, IF_PAD), F32),
                   jax.ShapeDtypeStruct((IF_ROWS, s), F32)],
        compiler_params=_params("parallel"),
        name="norm_if",
    )(x, g, wif, wift)


def _norm(x, g, tm):
    s, d = x.shape
    return pl.pallas_call(
        _norm_kernel,
        grid=(s // tm,),
        in_specs=[pl.BlockSpec((tm, d), lambda i: (i, 0)),
                  pl.BlockSpec((1, d), lambda i: (0, 0))],
        out_specs=pl.BlockSpec((tm, d), lambda i: (i, 0)),
        out_shape=jax.ShapeDtypeStruct((s, d), BF16),
        compiler_params=_params("parallel"),
        name="norm",
    )(x, g)


def _mm_kernel(x_ref, w_ref, o_ref):
    o_ref[...] = _dot(x_ref[...], w_ref[...]).astype(o_ref.dtype)


def _mm_res_kernel(x_ref, w_ref, r_ref, o_ref):
    o_ref[...] = r_ref[...] + _dot(x_ref[...], w_ref[...])


def _matmul(x, w, out_dtype, tm, tn, name):
    m, k = x.shape
    n = w.shape[1]
    return pl.pallas_call(
        _mm_kernel,
        grid=(m // tm, n // tn),
        in_specs=[pl.BlockSpec((tm, k), lambda i, j: (i, 0)),
                  pl.BlockSpec((k, tn), lambda i, j: (0, j))],
        out_specs=pl.BlockSpec((tm, tn), lambda i, j: (i, j)),
        out_shape=jax.ShapeDtypeStruct((m, n), out_dtype),
        compiler_params=_params("parallel", "parallel"),
        name=name,
    )(x, w)


def _matmul_residual(x, w_stack, layer, res, tm, tn, name):
    m, k = x.shape
    n = w_stack.shape[2]
    return pl.pallas_call(
        _mm_res_kernel,
        grid=(m // tm, n // tn),
        in_specs=[pl.BlockSpec((tm, k), lambda i, j: (i, 0)),
                  pl.BlockSpec((None, k, tn), lambda i, j: (layer, 0, j)),
                  pl.BlockSpec((tm, tn), lambda i, j: (i, j))],
        out_specs=pl.BlockSpec((tm, tn), lambda i, j: (i, j)),
        out_shape=jax.ShapeDtypeStruct((m, n), F32),
        compiler_params=_params("parallel", "parallel"),
        name=name,
    )(x, w_stack, res)


def _cast_at_first_row_tile(pairs):
    @pl.when(pl.program_id(1) == 0)
    def _():
        for w_ref, wb_s in pairs:
            wb_s[...] = w_ref[...].astype(BF16)


def _mm_ws_kernel(x_ref, w_ref, o_ref, wb_s):
    _cast_at_first_row_tile([(w_ref, wb_s)])
    o_ref[...] = _dot(x_ref[...], wb_s[...]).astype(o_ref.dtype)


def _mm_ws_res_kernel(x_ref, w_ref, r_ref, o_ref, wb_s):
    _cast_at_first_row_tile([(w_ref, wb_s)])
    o_ref[...] = r_ref[...] + _dot(x_ref[...], wb_s[...])


def _matmul_ws(x, w_stack, layer, n, out_dtype, tm, tn, name, res=None):
    m, k = x.shape
    in_specs = [pl.BlockSpec((tm, k), lambda j, i: (i, 0)),
                pl.BlockSpec((None, k, tn), lambda j, i: (layer, 0, j))]
    args = [x, w_stack]
    if res is not None:
        in_specs.append(pl.BlockSpec((tm, tn), lambda j, i: (i, j)))
        args.append(res)
    return pl.pallas_call(
        _mm_ws_kernel if res is None else _mm_ws_res_kernel,
        grid=(n // tn, m // tm),
        in_specs=in_specs,
        out_specs=pl.BlockSpec((tm, tn), lambda j, i: (i, j)),
        out_shape=jax.ShapeDtypeStruct((m, n), out_dtype),
        scratch_shapes=[pltpu.VMEM((k, tn), BF16)],
        compiler_params=_params("arbitrary", "arbitrary"),
        name=name,
    )(*args)


def _merge_kernel(xn_ref, ya_ref, yb_ref, yc_ref, yd_ref, wg_ref, bg_ref, wb_ref, o_ref, wg_s, wb_s):
    _cast_at_first_row_tile([(wg_ref, wg_s), (wb_ref, wb_s)])
    xn = xn_ref[...]
    acc = None
    for g, y_ref in enumerate((ya_ref, yb_ref, yc_ref, yd_ref)):
        gate = jax.nn.sigmoid(_dot(xn, wg_s[g]) + bg_ref[g:g + 1, :])
        term = gate * _dot(y_ref[...], wb_s[g])
        acc = term if acc is None else acc + term
    o_ref[...] = acc.astype(o_ref.dtype)


def _merge(xn, ys, wg_stack, bg_stack, wb_stack, layer, tm, tn):
    s, d = xn.shape
    w = ys[0].shape[1]
    y_spec = pl.BlockSpec((tm, w), lambda j, i: (i, 0))
    return pl.pallas_call(
        _merge_kernel,
        grid=(d // tn, s // tm),
        in_specs=[pl.BlockSpec((tm, d), lambda j, i: (i, 0)),
                  y_spec, y_spec, y_spec, y_spec,
                  pl.BlockSpec((None, N_BRANCH, d, tn), lambda j, i: (layer, 0, 0, j)),
                  pl.BlockSpec((None, N_BRANCH, tn), lambda j, i: (layer, 0, j)),
                  pl.BlockSpec((None, N_BRANCH, w, tn), lambda j, i: (layer, 0, 0, j))],
        out_specs=pl.BlockSpec((tm, tn), lambda j, i: (i, j)),
        out_shape=jax.ShapeDtypeStruct((s, d), BF16),
        scratch_shapes=[pltpu.VMEM((N_BRANCH, d, tn), BF16), pltpu.VMEM((N_BRANCH, w, tn), BF16)],
        compiler_params=_params("arbitrary", "arbitrary"),
        name="merge",
    )(xn, *ys, wg_stack, bg_stack, wb_stack)


def _ffn_up_kernel(x_ref, wg_ref, wu_ref, o_ref, wg_s, wu_s):
    _cast_at_first_row_tile([(wg_ref, wg_s), (wu_ref, wu_s)])
    x = x_ref[...]
    o_ref[...] = (jax.nn.silu(_dot(x, wg_s[...])) * _dot(x, wu_s[...])).astype(o_ref.dtype)


def _ffn_up(hn, wg_stack, wu_stack, layer, tm, tn):
    s, d = hn.shape
    f = wg_stack.shape[2]
    w_spec = pl.BlockSpec((None, d, tn), lambda j, i: (layer, 0, j))
    return pl.pallas_call(
        _ffn_up_kernel,
        grid=(f // tn, s // tm),
        in_specs=[pl.BlockSpec((tm, d), lambda j, i: (i, 0)), w_spec, w_spec],
        out_specs=pl.BlockSpec((tm, tn), lambda j, i: (i, j)),
        out_shape=jax.ShapeDtypeStruct((s, f), BF16),
        scratch_shapes=[pltpu.VMEM((d, tn), BF16), pltpu.VMEM((d, tn), BF16)],
        compiler_params=_params("arbitrary", "arbitrary"),
        name="ffn_up",
    )(hn, wg_stack, wu_stack)


def _col_spec(t, col):
    return pl.BlockSpec((t, BRANCH_W), lambda i: (i, col))


def _halo_spec(t, col):
    return pl.BlockSpec((8, BRANCH_W), lambda i: (jnp.maximum(i * (t // 8) - 1, 0), col))


def _full_spec(shape):
    nd = len(shape)
    return pl.BlockSpec(shape, lambda i: (0,) * nd)


def _shift_rows(x, prev8, j):
    xr = pltpu.roll(x, j, 0)
    pr = pltpu.roll(prev8, j, 0)
    rid = lax.broadcasted_iota(jnp.int32, pr.shape, 0)
    top = jnp.where(rid < j, pr, xr[:8])
    return jnp.concatenate([top, xr[8:]], axis=0)


def _causal_conv(x, prev8, w_ref, b_ref):
    acc = None
    for k in range(CONV_W):
        j = CONV_W - 1 - k
        xs = x if j == 0 else _shift_rows(x, prev8, j)
        term = xs * w_ref[k:k + 1, :]
        acc = term if acc is None else acc + term
    return acc + b_ref[...]


def _gmlp_kernel(u_ref, v_ref, gv_ref, ws_ref, bst_ref, o_ref):
    t = u_ref.shape[0]
    u = jax.nn.gelu(u_ref[...])
    v = (_rms(jax.nn.gelu(v_ref[...])) * gv_ref[...]).astype(BF16)
    row = lax.broadcasted_iota(jnp.int32, (CHUNK, CHUNK), 0)
    col = lax.broadcasted_iota(jnp.int32, (CHUNK, CHUNK), 1)
    for g in range(N_GROUPS):
        gs = slice(g * GROUP_DIM, (g + 1) * GROUP_DIM)
        ws = jnp.where(col <= row, ws_ref[g], 0.0).astype(BF16)
        bcol = bst_ref[:, g:g + 1]
        for c in range(t // CHUNK):
            cs = slice(c * CHUNK, (c + 1) * CHUNK)
            sp = _dot(ws, v[cs, gs]) + bcol
            o_ref[cs, gs] = (u[cs, gs] * sp).astype(o_ref.dtype)


def _gmlp(proj, gv, ws, bst, t):
    s = proj.shape[0]
    return pl.pallas_call(
        _gmlp_kernel,
        grid=(s // t,),
        in_specs=[_col_spec(t, COL_A_U), _col_spec(t, COL_A_V),
                  _full_spec((1, BRANCH_W)), _full_spec((N_GROUPS, CHUNK, CHUNK)),
                  _full_spec((CHUNK, N_GROUPS))],
        out_specs=pl.BlockSpec((t, BRANCH_W), lambda i: (i, 0)),
        out_shape=jax.ShapeDtypeStruct((s, BRANCH_W), BF16),
        compiler_params=_params("parallel"),
        name="gmlp",
    )(proj, proj, gv, ws, bst)


def _lin_scan(a, b):
    t = a.shape[0]
    rid = lax.broadcasted_iota(jnp.int32, a.shape, 0)
    d = 1
    while d < t:
        keep = rid >= d
        a_sh = jnp.where(keep, pltpu.roll(a, d, 0), 1.0)
        b_sh = jnp.where(keep, pltpu.roll(b, d, 0), 0.0)
        b = a * b_sh + b
        a = a * a_sh
        d *= 2
    return a, b


def _rglru_kernel(x_ref, xp_ref, gt_ref, cw_ref, cb_ref, wr_ref, br_ref, wi_ref, bi_ref, lam_ref,
                  o_ref, h_s):
    i = pl.program_id(0)

    @pl.when(i == 0)
    def _():
        h_s[...] = jnp.zeros_like(h_s)

    prev = jnp.where(i == 0, 0.0, xp_ref[...])
    xc = _causal_conv(x_ref[...], prev, cw_ref, cb_ref)
    xcb = xc.astype(BF16)
    r_parts, i_parts = [], []
    for g in range(N_GROUPS):
        gs = slice(g * GROUP_DIM, (g + 1) * GROUP_DIM)
        r_parts.append(_dot(xcb[:, gs], wr_ref[g]))
        i_parts.append(_dot(xcb[:, gs], wi_ref[g]))
    r = jax.nn.sigmoid(jnp.concatenate(r_parts, axis=1) + br_ref[...])
    ig = jax.nn.sigmoid(jnp.concatenate(i_parts, axis=1) + bi_ref[...])
    log_a = LRU_C * r * _log_sigmoid(lam_ref[...])
    a = jnp.exp(log_a)
    bx = jnp.sqrt(-jnp.tanh(log_a) * (a * a + 1.0)) * (ig * xc)
    a_cum, h_loc = _lin_scan(a, bx)
    h = h_loc + a_cum * h_s[0:1, :]
    t = h.shape[0]
    h_s[...] = jnp.broadcast_to(h[t - 1:t, :], h_s.shape)
    o_ref[...] = (h * jax.nn.gelu(gt_ref[...])).astype(o_ref.dtype)


def _rglru(proj, cw, cb, wr, br, wi, bi, lam, t):
    s = proj.shape[0]
    return pl.pallas_call(
        _rglru_kernel,
        grid=(s // t,),
        in_specs=[_col_spec(t, COL_B_X), _halo_spec(t, COL_B_X), _col_spec(t, COL_B_G),
                  _full_spec((CONV_W, BRANCH_W)), _full_spec((1, BRANCH_W)),
                  _full_spec((N_GROUPS, GROUP_DIM, GROUP_DIM)), _full_spec((1, BRANCH_W)),
                  _full_spec((N_GROUPS, GROUP_DIM, GROUP_DIM)), _full_spec((1, BRANCH_W)),
                  _full_spec((1, BRANCH_W))],
        out_specs=pl.BlockSpec((t, BRANCH_W), lambda i: (i, 0)),
        out_shape=jax.ShapeDtypeStruct((s, BRANCH_W), BF16),
        scratch_shapes=[pltpu.VMEM((8, BRANCH_W), F32)],
        compiler_params=_params("arbitrary"),
        name="rglru",
    )(proj, proj, proj, cw, cb, wr, br, wi, bi, lam)


def _mlstm_kernel(q_ref, qp_ref, k_ref, kp_ref, v_ref, og_ref, ifc_ref, ifr_ref,
                  cwq_ref, cbq_ref, cwk_ref, cbk_ref, bifc_ref, bifr_ref, gh_ref,
                  y_ref, ct_s, n_s, m_s):
    i = pl.program_id(0)

    @pl.when(i == 0)
    def _():
        ct_s[...] = jnp.zeros_like(ct_s)
        n_s[...] = jnp.zeros_like(n_s)
        m_s[...] = jnp.zeros_like(m_s)

    first = i == 0
    q = jax.nn.silu(_causal_conv(q_ref[...], jnp.where(first, 0.0, qp_ref[...]), cwq_ref, cbq_ref))
    k = jax.nn.silu(_causal_conv(k_ref[...], jnp.where(first, 0.0, kp_ref[...]), cwk_ref, cbk_ref))
    k = k * (GROUP_DIM ** -0.5)
    v = v_ref[...]
    og = og_ref[...]

    ifc = ifc_ref[...] + bifc_ref[...]
    ifr = ifr_ref[...] + bifr_ref[...]
    row = lax.broadcasted_iota(jnp.int32, (CHUNK, CHUNK), 0)
    col = lax.broadcasted_iota(jnp.int32, (CHUNK, CHUNK), 1)
    tril = col <= row
    tril_b = jnp.where(tril, 1.0, 0.0).astype(BF16)
    triu_b = jnp.where(row <= col, 1.0, 0.0).astype(BF16)
    bcc = sum(_dot(tril_b, piece) for piece in _split3(_log_sigmoid(ifc)))
    bcr = sum(_dot(piece, triu_b) for piece in _split3(_log_sigmoid(ifr)))

    for h in range(N_GROUPS):
        hs = slice(h * GROUP_DIM, (h + 1) * GROUP_DIM)
        qh, kh, vh = q[:, hs], k[:, hs], v[:, hs]
        ic_row, ic_col = ifr[h:h + 1, :], ifc[:, h:h + 1]
        bc_row, bc_col = bcr[N_GROUPS + h:N_GROUPS + h + 1, :], bcc[:, N_GROUPS + h:N_GROUPS + h + 1]
        m_st = m_s[h:h + 1, 0:1]
        n_row = n_s[h:h + 1, :]
        ct = ct_s[h]

        dmat = jnp.where(tril, bc_col - bc_row + ic_row, -jnp.inf)
        m_inter = bc_col + m_st
        m_t = jnp.maximum(m_inter, jnp.max(dmat, axis=-1, keepdims=True))
        kt = kh.T
        qb, vb = qh.astype(BF16), vh.astype(BF16)
        p = _dot(qb, kt.astype(BF16)) * jnp.exp(dmat - m_t)
        sc = jnp.exp(m_inter - m_t)
        num = _dot(p.astype(BF16), vb) + sc * _dot(qb, ct.astype(BF16))
        den = jnp.sum(p, axis=-1, keepdims=True) + sc * jnp.sum(qh * n_row, axis=-1, keepdims=True)
        hh = num / jnp.maximum(jnp.abs(den), jnp.exp(-m_t))

        b_last = bc_row[:, CHUNK - 1:CHUNK]
        g_row = b_last - bc_row + ic_row
        g_col = b_last - bc_col + ic_col
        m_new = jnp.maximum(b_last + m_st, jnp.max(g_row, axis=-1, keepdims=True))
        w_row = jnp.exp(g_row - m_new)
        w_col = jnp.exp(g_col - m_new)
        decay = jnp.exp(b_last + m_st - m_new)
        ct_s[h] = decay * ct + _dot((kt * w_row).astype(BF16), vb)
        n_s[h:h + 1, :] = decay * n_row + jnp.sum(kh * w_col, axis=0, keepdims=True)
        m_s[h:h + 1, :] = jnp.broadcast_to(m_new, (1, GROUP_DIM))

        hn = _rms(hh) * gh_ref[:, hs]
        y_ref[:, hs] = (jax.nn.sigmoid(og[:, hs]) * hn).astype(y_ref.dtype)


def _mlstm(proj, ifc, ifr, cwq, cbq, cwk, cbk, bifc, bifr, gh):
    s = proj.shape[0]
    t = CHUNK
    return pl.pallas_call(
        _mlstm_kernel,
        grid=(s // t,),
        in_specs=[_col_spec(t, COL_C_Q), _halo_spec(t, COL_C_Q),
                  _col_spec(t, COL_C_K), _halo_spec(t, COL_C_K),
                  _col_spec(t, COL_C_V), _col_spec(t, COL_C_O),
                  pl.BlockSpec((t, IF_PAD), lambda i: (i, 0)),
                  pl.BlockSpec((IF_ROWS, t), lambda i: (0, i)),
                  _full_spec((CONV_W, BRANCH_W)), _full_spec((1, BRANCH_W)),
                  _full_spec((CONV_W, BRANCH_W)), _full_spec((1, BRANCH_W)),
                  _full_spec((1, IF_PAD)), _full_spec((IF_ROWS, 1)),
                  _full_spec((1, BRANCH_W))],
        out_specs=pl.BlockSpec((t, BRANCH_W), lambda i: (i, 0)),
        out_shape=jax.ShapeDtypeStruct((s, BRANCH_W), BF16),
        scratch_shapes=[pltpu.VMEM((N_GROUPS, GROUP_DIM, GROUP_DIM), F32),
                        pltpu.VMEM((8, GROUP_DIM), F32),
                        pltpu.VMEM((8, GROUP_DIM), F32)],
        compiler_params=_params("arbitrary"),
        name="mlstm",
    )(proj, proj, proj, proj, proj, proj, ifc, ifr, cwq, cbq, cwk, cbk, bifc, bifr, gh)


def _sb_prep_kernel(q_ref, k_ref, v_ref, gq_ref, gk_ref, qo_ref, ko_ref, vo_ref):
    q, k, v = q_ref[...], k_ref[...], v_ref[...]
    for h in range(N_GROUPS):
        hs = slice(h * GROUP_DIM, (h + 1) * GROUP_DIM)
        qo_ref[h] = (_rms(q[:, hs]) * gq_ref[...] * (GROUP_DIM ** -0.5 * LOG2E)).astype(BF16)
        ko_ref[h] = (_rms(k[:, hs]) * gk_ref[...]).astype(BF16)
        vo_ref[h] = v[:, hs].astype(BF16)


def _sb_prep(proj, gq, gk, t):
    s = proj.shape[0]
    hm_spec = pl.BlockSpec((N_GROUPS, t, GROUP_DIM), lambda i: (0, i, 0))
    hm_shape = jax.ShapeDtypeStruct((N_GROUPS, s, GROUP_DIM), BF16)
    return pl.pallas_call(
        _sb_prep_kernel,
        grid=(s // t,),
        in_specs=[_col_spec(t, COL_D_Q), _col_spec(t, COL_D_K), _col_spec(t, COL_D_V),
                  _full_spec((1, GROUP_DIM)), _full_spec((1, GROUP_DIM))],
        out_specs=[hm_spec, hm_spec, hm_spec],
        out_shape=[hm_shape, hm_shape, hm_shape],
        compiler_params=_params("parallel"),
        name="sb_prep",
    )(proj, proj, proj, gq, gk)


def _sb_kernel(q_ref, k_ref, v_ref, o_ref, *, tq, sub):
    qi = pl.program_id(1)
    nsub = tq // sub
    q = q_ref[0]
    krow = lax.broadcasted_iota(jnp.int32, (sub, sub), 0)
    kcol = lax.broadcasted_iota(jnp.int32, (sub, sub), 1)
    from_j = jnp.where(krow >= kcol, 1.0, 0.0).astype(BF16)
    from_j2 = jnp.concatenate([from_j, from_j], axis=0)

    def piece(qa, start, strict, run):
        kb = k_ref[0, pl.ds(start, sub), :]
        z = _dot_nt(qa, kb)
        neg_abs = lax.bitcast_convert_type(
            lax.bitcast_convert_type(z, jnp.uint32) | jnp.uint32(0x80000000), F32)
        sp = jnp.maximum(z, 0.0) + jnp.log(1.0 + jnp.exp2(neg_abs)) * LOG2E
        if strict is not None:
            sp = jnp.where(strict, sp, 0.0)
        hi = sp.astype(BF16)
        lo = (sp - hi.astype(F32)).astype(BF16)
        att = jnp.exp2(z - _dot(jnp.concatenate([hi, lo], axis=1), from_j2) - run)
        if strict is not None:
            att = jnp.where(strict, att, 0.0)
        return att.astype(BF16), run + jnp.sum(sp, axis=-1, keepdims=True)

    acc = jnp.zeros((tq, GROUP_DIM), F32)
    run = jnp.zeros((tq, 1), F32)
    for b in range(nsub - 1, -1, -1):
        r0 = b * sub
        qpos = lax.broadcasted_iota(jnp.int32, (tq - r0, sub), 0)
        kpos = lax.broadcasted_iota(jnp.int32, (tq - r0, sub), 1)
        start = pl.multiple_of(qi * tq + r0, sub)
        att, run_b = piece(q[r0:], start, kpos < qpos, run[r0:])
        acc_b = acc[r0:] + _dot(att, v_ref[0, pl.ds(start, sub), :])
        acc = acc_b if b == 0 else jnp.concatenate([acc[:r0], acc_b], axis=0)
        run = run_b if b == 0 else jnp.concatenate([run[:r0], run_b], axis=0)

    def tile(j, carry):
        acc, run = carry
        base = (qi - 1 - j) * tq
        atts = []
        for b in range(nsub - 1, -1, -1):
            att, run = piece(q, pl.multiple_of(base + b * sub, sub), None, run)
            atts.append(att)
        vb = v_ref[0, pl.ds(pl.multiple_of(base, tq), tq), :]
        return acc + _dot(jnp.concatenate(atts[::-1], axis=1), vb), run

    acc, run = lax.fori_loop(0, qi, tile, (acc, run))
    o_ref[...] = acc.astype(o_ref.dtype)


def _sb_attention(qh, kh, vh, tq):
    _, s, _ = qh.shape
    return pl.pallas_call(
        functools.partial(_sb_kernel, tq=tq, sub=min(tq, 256)),
        grid=(N_GROUPS, s // tq),
        in_specs=[pl.BlockSpec((1, tq, GROUP_DIM), lambda h, i: (h, i, 0)),
                  pl.BlockSpec((1, s, GROUP_DIM), lambda h, i: (h, 0, 0)),
                  pl.BlockSpec((1, s, GROUP_DIM), lambda h, i: (h, 0, 0))],
        out_specs=pl.BlockSpec((tq, GROUP_DIM), lambda h, i: (i, h)),
        out_shape=jax.ShapeDtypeStruct((s, BRANCH_W), BF16),
        compiler_params=_params("parallel", "parallel"),
        name="sb_attn",
    )(qh, kh, vh)


def _tile(s, pref):
    return min(s, pref)


def _layer(x, p):
    s = x.shape[0]
    l, st = p["layer"], p["stacks"]
    xn, ifc, ifr = _norm_if(x, p["norm_mix"], p["w_if"], p["w_if_t"], _tile(s, 512))
    proj = _matmul_ws(xn, st["w_in"], l, 8 * BRANCH_W, F32, _tile(s, 1024), 512, "in_proj")
    proj_d = _matmul(xn, p["w_in_d"], F32, _tile(s, 1024), 512, "in_proj_d")

    y_a = _gmlp(proj, p["gm_norm_v"], p["gm_w_s"], p["gm_b_s_t"], _tile(s, 512))
    y_b = _rglru(proj, p["lru_conv_w"], p["lru_conv_b"], p["lru_w_r"], p["lru_b_r"],
                 p["lru_w_i"], p["lru_b_i"], p["lru_lambda"], _tile(s, 512))
    y_c = _mlstm(proj, ifc, ifr, p["ml_conv_w_q"], p["ml_conv_b_q"], p["ml_conv_w_k"], p["ml_conv_b_k"],
                 p["ml_b_if_c"], p["ml_b_if_r"], p["ml_norm_h"])
    qh, kh, vh = _sb_prep(proj_d, p["sb_norm_q"], p["sb_norm_k"], _tile(s, 512))
    y_d = _sb_attention(qh, kh, vh, _tile(s, 1024))

    merged = _merge(xn, (y_a, y_b, y_c, y_d), st["w_gate"], st["b_gate"], st["w_branch"], l, _tile(s, 1024), 256)
    x = _matmul_ws(merged, st["w_out"], l, D_MODEL, F32, _tile(s, 1024), 512, "out_proj", res=x)
    hn = _norm(x, p["norm_ffn"], _tile(s, 512))
    hid = _ffn_up(hn, st["w_ffn_gate"], st["w_ffn_up"], l, _tile(s, 1024), 512)
    return _matmul_residual(hid, st["w_ffn_down_bf16"], l, x, _tile(s, 512), 512, "ffn_down")


def _prepare_layer(l, stacks, norm_mix, w_in, gm_norm_v, gm_w_s, gm_b_s, lru_conv_w, lru_conv_b, lru_w_r, lru_b_r,
                   lru_w_i, lru_b_i, lru_lambda, ml_conv_w, ml_conv_b, ml_b_i, ml_b_f, ml_norm_h, sb_norm_q,
                   sb_norm_k, w_branch, w_gate, b_gate, w_out, norm_ffn, w_ffn_gate, w_ffn_up, w_ffn_down):
    w = BRANCH_W
    n_main = 8 * w
    w_if = w_in[l, :, n_main:n_main + 2 * N_GROUPS].astype(BF16)
    w_if_c = jnp.pad(w_if, ((0, 0), (0, IF_PAD - 2 * N_GROUPS)))
    w_if_r = jnp.pad(w_if.T, ((0, IF_ROWS - 2 * N_GROUPS), (0, 0)))
    b_if = jnp.concatenate([ml_b_i[l], ml_b_f[l]])
    return {
        "layer": l, "stacks": stacks,
        "norm_mix": norm_mix[l][None, :],
        "w_in_d": w_in[l, :, n_main + 2 * N_GROUPS:].astype(BF16), "w_if": w_if_c, "w_if_t": w_if_r,
        "gm_norm_v": gm_norm_v[l][None, :],
        "gm_w_s": gm_w_s[l],
        "gm_b_s_t": gm_b_s[l].T,
        "lru_conv_w": lru_conv_w[l], "lru_conv_b": lru_conv_b[l][None, :],
        "lru_w_r": lru_w_r[l].astype(BF16), "lru_b_r": lru_b_r[l][None, :],
        "lru_w_i": lru_w_i[l].astype(BF16), "lru_b_i": lru_b_i[l][None, :],
        "lru_lambda": lru_lambda[l][None, :],
        "ml_conv_w_q": ml_conv_w[l][:, :w], "ml_conv_b_q": ml_conv_b[l][None, :w],
        "ml_conv_w_k": ml_conv_w[l][:, w:], "ml_conv_b_k": ml_conv_b[l][None, w:],
        "ml_b_if_c": jnp.pad(b_if, (0, IF_PAD - 2 * N_GROUPS))[None, :],
        "ml_b_if_r": jnp.pad(b_if, (0, IF_ROWS - 2 * N_GROUPS))[:, None],
        "ml_norm_h": ml_norm_h[l].reshape(1, w),
        "sb_norm_q": sb_norm_q[l][None, :], "sb_norm_k": sb_norm_k[l][None, :],
        "norm_ffn": norm_ffn[l][None, :],
    }


def kernel(x, norm_mix, w_in, gm_norm_v, gm_w_s, gm_b_s, lru_conv_w, lru_conv_b, lru_w_r, lru_b_r, lru_w_i, lru_b_i, lru_lambda, ml_conv_w, ml_conv_b, ml_b_i, ml_b_f, ml_norm_h, sb_norm_q, sb_norm_k, w_branch, w_gate, b_gate, w_out, norm_ffn, w_ffn_gate, w_ffn_up, w_ffn_down):
    b, s, d = x.shape
    stacks = {"w_in": w_in, "w_gate": w_gate, "b_gate": b_gate, "w_branch": w_branch, "w_out": w_out,
              "w_ffn_gate": w_ffn_gate, "w_ffn_up": w_ffn_up, "w_ffn_down_bf16": w_ffn_down.astype(BF16)}
    outs = []
    for bi in range(b):
        xb = x.reshape(s, d) if b == 1 else x[bi]
        for l in range(norm_mix.shape[0]):
            p = _prepare_layer(l, stacks, norm_mix, w_in, gm_norm_v, gm_w_s, gm_b_s, lru_conv_w, lru_conv_b,
                               lru_w_r, lru_b_r, lru_w_i, lru_b_i, lru_lambda, ml_conv_w, ml_conv_b, ml_b_i,
                               ml_b_f, ml_norm_h, sb_norm_q, sb_norm_k, w_branch, w_gate, b_gate, w_out,
                               norm_ffn, w_ffn_gate, w_ffn_up, w_ffn_down)
            xb = _layer(xb, p)
        outs.append(xb)
    return outs[0].reshape(b, s, d) if b == 1 else jnp.stack(outs, axis=0)
```

```python
import functools

import jax
import jax.numpy as jnp
from jax import lax
from jax.experimental import pallas as pl
from jax.experimental.pallas import tpu as pltpu

D_MODEL = 2048
N_BRANCH = 4
BRANCH_W = D_MODEL // N_BRANCH
GROUP_DIM = 128
N_GROUPS = BRANCH_W // GROUP_DIM
CHUNK = 128
CONV_W = 4
LRU_C = 8.0
EPS = 1e-6
IF_PAD = 128
IF_ROWS = 16
LOG2E = 1.4426950408889634

F32 = jnp.float32
BF16 = jnp.bfloat16

VMEM_LIMIT_BYTES = 48 * 1024 * 1024

COL_A_U, COL_A_V, COL_B_X, COL_B_G, COL_C_Q, COL_C_K, COL_C_V, COL_C_O = range(8)
COL_D_Q, COL_D_K, COL_D_V = range(3)


def _params(*sem):
    return pltpu.CompilerParams(dimension_semantics=sem, vmem_limit_bytes=VMEM_LIMIT_BYTES)


def _dot(a, b):
    return jnp.dot(a, b, preferred_element_type=F32)


def _dot_nt(a, b):
    return lax.dot_general(a, b, (((1,), (1,)), ((), ())), preferred_element_type=F32)


def _log_sigmoid(x):
    return jnp.minimum(x, 0.0) - jnp.log1p(jnp.exp(-jnp.abs(x)))


def _rms(x):
    return x * lax.rsqrt(jnp.mean(x * x, axis=-1, keepdims=True) + EPS)


def _split3(a):
    a1 = a.astype(BF16)
    r1 = a - a1.astype(F32)
    a2 = r1.astype(BF16)
    a3 = (r1 - a2.astype(F32)).astype(BF16)
    return a1, a2, a3


def _norm_if_kernel(x_ref, g_ref, wif_ref, wift_ref, xn_ref, ifc_ref, ifr_ref):
    xn = (_rms(x_ref[...]) * g_ref[...]).astype(BF16)
    xn_ref[...] = xn
    ifc_ref[...] = _dot(xn, wif_ref[...])
    ifr_ref[...] = _dot_nt(wift_ref[...], xn)


def _norm_kernel(x_ref, g_ref, xn_ref):
    xn_ref[...] = (_rms(x_ref[...]) * g_ref[...]).astype(BF16)


def _norm_if(x, g, wif, wift, tm):
    s, d = x.shape
    return pl.pallas_call(
        _norm_if_kernel,
        grid=(s // tm,),
        in_specs=[pl.BlockSpec((tm, d), lambda i: (i, 0)),
                  pl.BlockSpec((1, d), lambda i: (0, 0)),
                  pl.BlockSpec((d, IF_PAD), lambda i: (0, 0)),
                  pl.BlockSpec((IF_ROWS, d), lambda i: (0, 0))],
        out_specs=[pl.BlockSpec((tm, d), lambda i: (i, 0)),
                   pl.BlockSpec((tm, IF_PAD), lambda i: (i, 0)),
                   pl.BlockSpec((IF_ROWS, tm), lambda i: (0, i))],
        out_shape=[jax.ShapeDtypeStruct((s, d), BF16),
                   jax.ShapeDtypeStruct((s, IF_PAD), F32),
                   jax.ShapeDtypeStruct((IF_ROWS, s), F32)],
        compiler_params=_params("parallel"),
        name="norm_if",
    )(x, g, wif, wift)


def _norm(x, g, tm):
    s, d = x.shape
    return pl.pallas_call(
        _norm_kernel,
        grid=(s // tm,),
        in_specs=[pl.BlockSpec((tm, d), lambda i: (i, 0)),
                  pl.BlockSpec((1, d), lambda i: (0, 0))],
        out_specs=pl.BlockSpec((tm, d), lambda i: (i, 0)),
        out_shape=jax.ShapeDtypeStruct((s, d), BF16),
        compiler_params=_params("parallel"),
        name="norm",
    )(x, g)


def _mm_kernel(x_ref, w_ref, o_ref):
    o_ref[...] = _dot(x_ref[...], w_ref[...]).astype(o_ref.dtype)


def _mm_res_kernel(x_ref, w_ref, r_ref, o_ref):
    o_ref[...] = r_ref[...] + _dot(x_ref[...], w_ref[...])


def _matmul(x, w_stack, layer, out_dtype, tm, tn, name):
    m, k = x.shape
    n = w_stack.shape[2]
    return pl.pallas_call(
        _mm_kernel,
        grid=(m // tm, n // tn),
        in_specs=[pl.BlockSpec((tm, k), lambda i, j: (i, 0)),
                  pl.BlockSpec((None, k, tn), lambda i, j: (layer, 0, j))],
        out_specs=pl.BlockSpec((tm, tn), lambda i, j: (i, j)),
        out_shape=jax.ShapeDtypeStruct((m, n), out_dtype),
        compiler_params=_params("parallel", "parallel"),
        name=name,
    )(x, w_stack)


def _matmul_residual(x, w_stack, layer, res, tm, tn, name):
    m, k = x.shape
    n = w_stack.shape[2]
    return pl.pallas_call(
        _mm_res_kernel,
        grid=(m // tm, n // tn),
        in_specs=[pl.BlockSpec((tm, k), lambda i, j: (i, 0)),
                  pl.BlockSpec((None, k, tn), lambda i, j: (layer, 0, j)),
                  pl.BlockSpec((tm, tn), lambda i, j: (i, j))],
        out_specs=pl.BlockSpec((tm, tn), lambda i, j: (i, j)),
        out_shape=jax.ShapeDtypeStruct((m, n), F32),
        compiler_params=_params("parallel", "parallel"),
        name=name,
    )(x, w_stack, res)


def _cast_at_first_row_tile(pairs):
    @pl.when(pl.program_id(1) == 0)
    def _():
        for w_ref, wb_s in pairs:
            wb_s[...] = w_ref[...].astype(BF16)


def _mm_ws_kernel(x_ref, w_ref, o_ref, wb_s):
    _cast_at_first_row_tile([(w_ref, wb_s)])
    o_ref[...] = _dot(x_ref[...], wb_s[...]).astype(o_ref.dtype)


def _mm_ws_res_kernel(x_ref, w_ref, r_ref, o_ref, wb_s):
    _cast_at_first_row_tile([(w_ref, wb_s)])
    o_ref[...] = r_ref[...] + _dot(x_ref[...], wb_s[...])


def _matmul_ws(x, w_stack, layer, n, out_dtype, tm, tn, name, res=None):
    m, k = x.shape
    in_specs = [pl.BlockSpec((tm, k), lambda j, i: (i, 0)),
                pl.BlockSpec((None, k, tn), lambda j, i: (layer, 0, j))]
    args = [x, w_stack]
    if res is not None:
        in_specs.append(pl.BlockSpec((tm, tn), lambda j, i: (i, j)))
        args.append(res)
    return pl.pallas_call(
        _mm_ws_kernel if res is None else _mm_ws_res_kernel,
        grid=(n // tn, m // tm),
        in_specs=in_specs,
        out_specs=pl.BlockSpec((tm, tn), lambda j, i: (i, j)),
        out_shape=jax.ShapeDtypeStruct((m, n), out_dtype),
        scratch_shapes=[pltpu.VMEM((k, tn), BF16)],
        compiler_params=_params("arbitrary", "arbitrary"),
        name=name,
    )(*args)


def _merge_kernel(xn_ref, ya_ref, yb_ref, yc_ref, yd_ref, wg_ref, bg_ref, wb_ref, o_ref, wg_s, wb_s):
    _cast_at_first_row_tile([(wg_ref, wg_s), (wb_ref, wb_s)])
    xn = xn_ref[...]
    acc = None
    for g, y_ref in enumerate((ya_ref, yb_ref, yc_ref, yd_ref)):
        gate = jax.nn.sigmoid(_dot(xn, wg_s[g]) + bg_ref[g:g + 1, :])
        term = gate * _dot(y_ref[...], wb_s[g])
        acc = term if acc is None else acc + term
    o_ref[...] = acc.astype(o_ref.dtype)


def _merge(xn, ys, wg_stack, bg_stack, wb_stack, layer, tm, tn):
    s, d = xn.shape
    w = ys[0].shape[1]
    y_spec = pl.BlockSpec((tm, w), lambda j, i: (i, 0))
    return pl.pallas_call(
        _merge_kernel,
        grid=(d // tn, s // tm),
        in_specs=[pl.BlockSpec((tm, d), lambda j, i: (i, 0)),
                  y_spec, y_spec, y_spec, y_spec,
                  pl.BlockSpec((None, N_BRANCH, d, tn), lambda j, i: (layer, 0, 0, j)),
                  pl.BlockSpec((None, N_BRANCH, tn), lambda j, i: (layer, 0, j)),
                  pl.BlockSpec((None, N_BRANCH, w, tn), lambda j, i: (layer, 0, 0, j))],
        out_specs=pl.BlockSpec((tm, tn), lambda j, i: (i, j)),
        out_shape=jax.ShapeDtypeStruct((s, d), BF16),
        scratch_shapes=[pltpu.VMEM((N_BRANCH, d, tn), BF16), pltpu.VMEM((N_BRANCH, w, tn), BF16)],
        compiler_params=_params("arbitrary", "arbitrary"),
        name="merge",
    )(xn, *ys, wg_stack, bg_stack, wb_stack)


def _ffn_up_kernel(x_ref, wg_ref, wu_ref, o_ref, wg_s, wu_s):
    _cast_at_first_row_tile([(wg_ref, wg_s), (wu_ref, wu_s)])
    x = x_ref[...]
    o_ref[...] = (jax.nn.silu(_dot(x, wg_s[...])) * _dot(x, wu_s[...])).astype(o_ref.dtype)


def _ffn_up(hn, wg_stack, wu_stack, layer, tm, tn):
    s, d = hn.shape
    f = wg_stack.shape[2]
    w_spec = pl.BlockSpec((None, d, tn), lambda j, i: (layer, 0, j))
    return pl.pallas_call(
        _ffn_up_kernel,
        grid=(f // tn, s // tm),
        in_specs=[pl.BlockSpec((tm, d), lambda j, i: (i, 0)), w_spec, w_spec],
        out_specs=pl.BlockSpec((tm, tn), lambda j, i: (i, j)),
        out_shape=jax.ShapeDtypeStruct((s, f), BF16),
        scratch_shapes=[pltpu.VMEM((d, tn), BF16), pltpu.VMEM((d, tn), BF16)],
        compiler_params=_params("arbitrary", "arbitrary"),
        name="ffn_up",
    )(hn, wg_stack, wu_stack)


def _col_spec(t, col):
    return pl.BlockSpec((t, BRANCH_W), lambda i: (i, col))


def _halo_spec(t, col):
    return pl.BlockSpec((8, BRANCH_W), lambda i: (jnp.maximum(i * (t // 8) - 1, 0), col))


def _full_spec(shape):
    nd = len(shape)
    return pl.BlockSpec(shape, lambda i: (0,) * nd)


def _shift_rows(x, prev8, j):
    xr = pltpu.roll(x, j, 0)
    pr = pltpu.roll(prev8, j, 0)
    rid = lax.broadcasted_iota(jnp.int32, pr.shape, 0)
    top = jnp.where(rid < j, pr, xr[:8])
    return jnp.concatenate([top, xr[8:]], axis=0)


def _causal_conv(x, prev8, w_ref, b_ref):
    acc = None
    for k in range(CONV_W):
        j = CONV_W - 1 - k
        xs = x if j == 0 else _shift_rows(x, prev8, j)
        term = xs * w_ref[k:k + 1, :]
        acc = term if acc is None else acc + term
    return acc + b_ref[...]


def _gmlp_kernel(u_ref, v_ref, gv_ref, ws_ref, bst_ref, o_ref):
    t = u_ref.shape[0]
    u = jax.nn.gelu(u_ref[...])
    v = (_rms(jax.nn.gelu(v_ref[...])) * gv_ref[...]).astype(BF16)
    row = lax.broadcasted_iota(jnp.int32, (CHUNK, CHUNK), 0)
    col = lax.broadcasted_iota(jnp.int32, (CHUNK, CHUNK), 1)
    for g in range(N_GROUPS):
        gs = slice(g * GROUP_DIM, (g + 1) * GROUP_DIM)
        ws = jnp.where(col <= row, ws_ref[g], 0.0).astype(BF16)
        bcol = bst_ref[:, g:g + 1]
        for c in range(t // CHUNK):
            cs = slice(c * CHUNK, (c + 1) * CHUNK)
            sp = _dot(ws, v[cs, gs]) + bcol
            o_ref[cs, gs] = (u[cs, gs] * sp).astype(o_ref.dtype)


def _gmlp(proj, gv, ws, bst, t):
    s = proj.shape[0]
    return pl.pallas_call(
        _gmlp_kernel,
        grid=(s // t,),
        in_specs=[_col_spec(t, COL_A_U), _col_spec(t, COL_A_V),
                  _full_spec((1, BRANCH_W)), _full_spec((N_GROUPS, CHUNK, CHUNK)),
                  _full_spec((CHUNK, N_GROUPS))],
        out_specs=pl.BlockSpec((t, BRANCH_W), lambda i: (i, 0)),
        out_shape=jax.ShapeDtypeStruct((s, BRANCH_W), BF16),
        compiler_params=_params("parallel"),
        name="gmlp",
    )(proj, proj, gv, ws, bst)


def _lin_scan(a, b):
    t = a.shape[0]
    rid = lax.broadcasted_iota(jnp.int32, a.shape, 0)
    d = 1
    while d < t:
        keep = rid >= d
        a_sh = jnp.where(keep, pltpu.roll(a, d, 0), 1.0)
        b_sh = jnp.where(keep, pltpu.roll(b, d, 0), 0.0)
        b = a * b_sh + b
        a = a * a_sh
        d *= 2
    return a, b


def _rglru_kernel(x_ref, xp_ref, gt_ref, cw_ref, cb_ref, wr_ref, br_ref, wi_ref, bi_ref, lam_ref,
                  o_ref, h_s):
    i = pl.program_id(0)

    @pl.when(i == 0)
    def _():
        h_s[...] = jnp.zeros_like(h_s)

    prev = jnp.where(i == 0, 0.0, xp_ref[...])
    xc = _causal_conv(x_ref[...], prev, cw_ref, cb_ref)
    xcb = xc.astype(BF16)
    r_parts, i_parts = [], []
    for g in range(N_GROUPS):
        gs = slice(g * GROUP_DIM, (g + 1) * GROUP_DIM)
        r_parts.append(_dot(xcb[:, gs], wr_ref[g]))
        i_parts.append(_dot(xcb[:, gs], wi_ref[g]))
    r = jax.nn.sigmoid(jnp.concatenate(r_parts, axis=1) + br_ref[...])
    ig = jax.nn.sigmoid(jnp.concatenate(i_parts, axis=1) + bi_ref[...])
    log_a = LRU_C * r * _log_sigmoid(lam_ref[...])
    a = jnp.exp(log_a)
    bx = jnp.sqrt(-jnp.tanh(log_a) * (a * a + 1.0)) * (ig * xc)
    a_cum, h_loc = _lin_scan(a, bx)
    h = h_loc + a_cum * h_s[0:1, :]
    t = h.shape[0]
    h_s[...] = jnp.broadcast_to(h[t - 1:t, :], h_s.shape)
    o_ref[...] = (h * jax.nn.gelu(gt_ref[...])).astype(o_ref.dtype)


def _rglru(proj, cw, cb, wr, br, wi, bi, lam, t):
    s = proj.shape[0]
    return pl.pallas_call(
        _rglru_kernel,
        grid=(s // t,),
        in_specs=[_col_spec(t, COL_B_X), _halo_spec(t, COL_B_X), _col_spec(t, COL_B_G),
                  _full_spec((CONV_W, BRANCH_W)), _full_spec((1, BRANCH_W)),
                  _full_spec((N_GROUPS, GROUP_DIM, GROUP_DIM)), _full_spec((1, BRANCH_W)),
                  _full_spec((N_GROUPS, GROUP_DIM, GROUP_DIM)), _full_spec((1, BRANCH_W)),
                  _full_spec((1, BRANCH_W))],
        out_specs=pl.BlockSpec((t, BRANCH_W), lambda i: (i, 0)),
        out_shape=jax.ShapeDtypeStruct((s, BRANCH_W), BF16),
        scratch_shapes=[pltpu.VMEM((8, BRANCH_W), F32)],
        compiler_params=_params("arbitrary"),
        name="rglru",
    )(proj, proj, proj, cw, cb, wr, br, wi, bi, lam)


def _mlstm_kernel(q_ref, qp_ref, k_ref, kp_ref, v_ref, og_ref, ifc_ref, ifr_ref,
                  cwq_ref, cbq_ref, cwk_ref, cbk_ref, bifc_ref, bifr_ref, gh_ref,
                  y_ref, ct_s, n_s, m_s):
    i = pl.program_id(0)

    @pl.when(i == 0)
    def _():
        ct_s[...] = jnp.zeros_like(ct_s)
        n_s[...] = jnp.zeros_like(n_s)
        m_s[...] = jnp.zeros_like(m_s)

    first = i == 0
    q = jax.nn.silu(_causal_conv(q_ref[...], jnp.where(first, 0.0, qp_ref[...]), cwq_ref, cbq_ref))
    k = jax.nn.silu(_causal_conv(k_ref[...], jnp.where(first, 0.0, kp_ref[...]), cwk_ref, cbk_ref))
    k = k * (GROUP_DIM ** -0.5)
    v = v_ref[...]
    og = og_ref[...]

    row = lax.broadcasted_iota(jnp.int32, (CHUNK, CHUNK), 0)
    col = lax.broadcasted_iota(jnp.int32, (CHUNK, CHUNK), 1)
    tril = col <= row
    tril_b = jnp.where(tril, 1.0, 0.0).astype(BF16)
    triu_b = jnp.where(row <= col, 1.0, 0.0).astype(BF16)
    state = [(ct_s[h], n_s[h:h + 1, :], m_s[h:h + 1, 0:1]) for h in range(N_GROUPS)]

    for c in range(q.shape[0] // CHUNK):
        cs = slice(c * CHUNK, (c + 1) * CHUNK)
        ifc = ifc_ref[cs, :] + bifc_ref[...]
        ifr = ifr_ref[:, cs] + bifr_ref[...]
        bcc = sum(_dot(tril_b, piece) for piece in _split3(_log_sigmoid(ifc)))
        bcr = sum(_dot(piece, triu_b) for piece in _split3(_log_sigmoid(ifr)))

        for h in range(N_GROUPS):
            hs = slice(h * GROUP_DIM, (h + 1) * GROUP_DIM)
            qh, kh, vh = q[cs, hs], k[cs, hs], v[cs, hs]
            ic_row, ic_col = ifr[h:h + 1, :], ifc[:, h:h + 1]
            bc_row, bc_col = bcr[N_GROUPS + h:N_GROUPS + h + 1, :], bcc[:, N_GROUPS + h:N_GROUPS + h + 1]
            ct, n_row, m_st = state[h]

            dmat = jnp.where(tril, bc_col - bc_row + ic_row, -jnp.inf)
            m_inter = bc_col + m_st
            m_t = jnp.maximum(m_inter, jnp.max(dmat, axis=-1, keepdims=True))
            kt = kh.T
            qb, vb = qh.astype(BF16), vh.astype(BF16)
            p = _dot(qb, kt.astype(BF16)) * jnp.exp(dmat - m_t)
            sc = jnp.exp(m_inter - m_t)
            num = _dot(p.astype(BF16), vb) + sc * _dot(qb, ct.astype(BF16))
            den = jnp.sum(p, axis=-1, keepdims=True) + sc * jnp.sum(qh * n_row, axis=-1, keepdims=True)
            hh = num / jnp.maximum(jnp.abs(den), jnp.exp(-m_t))

            b_last = bc_row[:, CHUNK - 1:CHUNK]
            g_row = b_last - bc_row + ic_row
            g_col = b_last - bc_col + ic_col
            m_new = jnp.maximum(b_last + m_st, jnp.max(g_row, axis=-1, keepdims=True))
            w_row = jnp.exp(g_row - m_new)
            w_col = jnp.exp(g_col - m_new)
            decay = jnp.exp(b_last + m_st - m_new)
            state[h] = (decay * ct + _dot((kt * w_row).astype(BF16), vb),
                        decay * n_row + jnp.sum(kh * w_col, axis=0, keepdims=True),
                        m_new)

            hn = _rms(hh) * gh_ref[:, hs]
            y_ref[cs, hs] = (jax.nn.sigmoid(og[cs, hs]) * hn).astype(y_ref.dtype)

    for h in range(N_GROUPS):
        ct_s[h] = state[h][0]
        n_s[h:h + 1, :] = state[h][1]
        m_s[h:h + 1, :] = jnp.broadcast_to(state[h][2], (1, GROUP_DIM))


def _mlstm(proj, ifc, ifr, cwq, cbq, cwk, cbk, bifc, bifr, gh, t):
    s = proj.shape[0]
    return pl.pallas_call(
        _mlstm_kernel,
        grid=(s // t,),
        in_specs=[_col_spec(t, COL_C_Q), _halo_spec(t, COL_C_Q),
                  _col_spec(t, COL_C_K), _halo_spec(t, COL_C_K),
                  _col_spec(t, COL_C_V), _col_spec(t, COL_C_O),
                  pl.BlockSpec((t, IF_PAD), lambda i: (i, 0)),
                  pl.BlockSpec((IF_ROWS, t), lambda i: (0, i)),
                  _full_spec((CONV_W, BRANCH_W)), _full_spec((1, BRANCH_W)),
                  _full_spec((CONV_W, BRANCH_W)), _full_spec((1, BRANCH_W)),
                  _full_spec((1, IF_PAD)), _full_spec((IF_ROWS, 1)),
                  _full_spec((1, BRANCH_W))],
        out_specs=pl.BlockSpec((t, BRANCH_W), lambda i: (i, 0)),
        out_shape=jax.ShapeDtypeStruct((s, BRANCH_W), BF16),
        scratch_shapes=[pltpu.VMEM((N_GROUPS, GROUP_DIM, GROUP_DIM), F32),
                        pltpu.VMEM((8, GROUP_DIM), F32),
                        pltpu.VMEM((8, GROUP_DIM), F32)],
        compiler_params=_params("arbitrary"),
        name="mlstm",
    )(proj, proj, proj, proj, proj, proj, ifc, ifr, cwq, cbq, cwk, cbk, bifc, bifr, gh)


def _sb_prep_kernel(q_ref, k_ref, v_ref, gq_ref, gk_ref, qo_ref, ko_ref, vo_ref):
    q, k, v = q_ref[...], k_ref[...], v_ref[...]
    for h in range(N_GROUPS):
        hs = slice(h * GROUP_DIM, (h + 1) * GROUP_DIM)
        qo_ref[h] = (_rms(q[:, hs]) * gq_ref[...] * (GROUP_DIM ** -0.5 * LOG2E)).astype(BF16)
        ko_ref[h] = (_rms(k[:, hs]) * gk_ref[...]).astype(BF16)
        vo_ref[h] = v[:, hs].astype(BF16)


def _sb_prep(proj, gq, gk, t):
    s = proj.shape[0]
    hm_spec = pl.BlockSpec((N_GROUPS, t, GROUP_DIM), lambda i: (0, i, 0))
    hm_shape = jax.ShapeDtypeStruct((N_GROUPS, s, GROUP_DIM), BF16)
    return pl.pallas_call(
        _sb_prep_kernel,
        grid=(s // t,),
        in_specs=[_col_spec(t, COL_D_Q), _col_spec(t, COL_D_K), _col_spec(t, COL_D_V),
                  _full_spec((1, GROUP_DIM)), _full_spec((1, GROUP_DIM))],
        out_specs=[hm_spec, hm_spec, hm_spec],
        out_shape=[hm_shape, hm_shape, hm_shape],
        compiler_params=_params("parallel"),
        name="sb_prep",
    )(proj, proj, proj, gq, gk)


def _sb_kernel(q_ref, k_ref, v_ref, o_ref, *, tq, sub):
    qi = pl.program_id(1)
    nsub = tq // sub
    q = q_ref[0]
    krow = lax.broadcasted_iota(jnp.int32, (sub, sub), 0)
    kcol = lax.broadcasted_iota(jnp.int32, (sub, sub), 1)
    from_j = jnp.where(krow >= kcol, 1.0, 0.0).astype(BF16)

    def piece(qa, start, strict, run):
        kb = k_ref[0, pl.ds(start, sub), :]
        z = _dot_nt(qa, kb)
        neg_abs = lax.bitcast_convert_type(
            lax.bitcast_convert_type(z, jnp.uint32) | jnp.uint32(0x80000000), F32)
        sp = jnp.maximum(z, 0.0) + jnp.log(1.0 + jnp.exp2(neg_abs)) * LOG2E
        if strict is not None:
            sp = jnp.where(strict, sp, 0.0)
        att = jnp.exp2(z - _dot(sp.astype(BF16), from_j) - run)
        if strict is not None:
            att = jnp.where(strict, att, 0.0)
        return att.astype(BF16), run + jnp.sum(sp, axis=-1, keepdims=True)

    acc = jnp.zeros((tq, GROUP_DIM), F32)
    run = jnp.zeros((tq, 1), F32)
    for b in range(nsub - 1, -1, -1):
        r0 = b * sub
        qpos = lax.broadcasted_iota(jnp.int32, (tq - r0, sub), 0)
        kpos = lax.broadcasted_iota(jnp.int32, (tq - r0, sub), 1)
        start = pl.multiple_of(qi * tq + r0, sub)
        att, run_b = piece(q[r0:], start, kpos < qpos, run[r0:])
        acc_b = acc[r0:] + _dot(att, v_ref[0, pl.ds(start, sub), :])
        acc = acc_b if b == 0 else jnp.concatenate([acc[:r0], acc_b], axis=0)
        run = run_b if b == 0 else jnp.concatenate([run[:r0], run_b], axis=0)

    def tile(j, carry):
        acc, run = carry
        base = (qi - 1 - j) * tq
        atts = []
        for b in range(nsub - 1, -1, -1):
            att, run = piece(q, pl.multiple_of(base + b * sub, sub), None, run)
            atts.append(att)
        vb = v_ref[0, pl.ds(pl.multiple_of(base, tq), tq), :]
        return acc + _dot(jnp.concatenate(atts[::-1], axis=1), vb), run

    acc, run = lax.fori_loop(0, qi, tile, (acc, run))
    o_ref[...] = acc.astype(o_ref.dtype)


def _sb_attention(qh, kh, vh, tq):
    _, s, _ = qh.shape
    return pl.pallas_call(
        functools.partial(_sb_kernel, tq=tq, sub=min(tq, 256)),
        grid=(N_GROUPS, s // tq),
        in_specs=[pl.BlockSpec((1, tq, GROUP_DIM), lambda h, i: (h, i, 0)),
                  pl.BlockSpec((1, s, GROUP_DIM), lambda h, i: (h, 0, 0)),
                  pl.BlockSpec((1, s, GROUP_DIM), lambda h, i: (h, 0, 0))],
        out_specs=pl.BlockSpec((tq, GROUP_DIM), lambda h, i: (i, h)),
        out_shape=jax.ShapeDtypeStruct((s, BRANCH_W), BF16),
        compiler_params=_params("parallel", "parallel"),
        name="sb_attn",
    )(qh, kh, vh)


def _tile(s, pref):
    return min(s, pref)


def _layer(x, p):
    s = x.shape[0]
    l, st = p["layer"], p["stacks"]
    xn, ifc, ifr = _norm_if(x, p["norm_mix"], p["w_if"], p["w_if_t"], _tile(s, 512))
    proj = _matmul(xn, st["w_in_abc_bf16"], l, F32, _tile(s, 1024), 512, "in_proj")
    proj_d = _matmul(xn, st["w_in_d_bf16"], l, F32, _tile(s, 1024), 512, "in_proj_d")

    y_a = _gmlp(proj, p["gm_norm_v"], p["gm_w_s"], p["gm_b_s_t"], _tile(s, 512))
    y_b = _rglru(proj, p["lru_conv_w"], p["lru_conv_b"], p["lru_w_r"], p["lru_b_r"],
                 p["lru_w_i"], p["lru_b_i"], p["lru_lambda"], _tile(s, 512))
    y_c = _mlstm(proj, ifc, ifr, p["ml_conv_w_q"], p["ml_conv_b_q"], p["ml_conv_w_k"], p["ml_conv_b_k"],
                 p["ml_b_if_c"], p["ml_b_if_r"], p["ml_norm_h"], CHUNK)
    qh, kh, vh = _sb_prep(proj_d, p["sb_norm_q"], p["sb_norm_k"], _tile(s, 512))
    y_d = _sb_attention(qh, kh, vh, _tile(s, 1024))

    merged = _merge(xn, (y_a, y_b, y_c, y_d), st["w_gate"], st["b_gate"], st["w_branch"], l, _tile(s, 1024), 256)
    x = _matmul_ws(merged, st["w_out"], l, D_MODEL, F32, _tile(s, 1024), 512, "out_proj", res=x)
    hn = _norm(x, p["norm_ffn"], _tile(s, 512))
    hid = _ffn_up(hn, st["w_ffn_gate"], st["w_ffn_up"], l, _tile(s, 1024), 512)
    return _matmul_residual(hid, st["w_ffn_down_bf16"], l, x, _tile(s, 512), 512, "ffn_down")


def _prepare_layer(l, stacks, norm_mix, w_in, gm_norm_v, gm_w_s, gm_b_s, lru_conv_w, lru_conv_b, lru_w_r, lru_b_r,
                   lru_w_i, lru_b_i, lru_lambda, ml_conv_w, ml_conv_b, ml_b_i, ml_b_f, ml_norm_h, sb_norm_q,
                   sb_norm_k, w_branch, w_gate, b_gate, w_out, norm_ffn, w_ffn_gate, w_ffn_up, w_ffn_down):
    w = BRANCH_W
    n_main = 8 * w
    w_if = w_in[l, :, n_main:n_main + 2 * N_GROUPS].astype(BF16)
    w_if_c = jnp.pad(w_if, ((0, 0), (0, IF_PAD - 2 * N_GROUPS)))
    w_if_r = jnp.pad(w_if.T, ((0, IF_ROWS - 2 * N_GROUPS), (0, 0)))
    b_if = jnp.concatenate([ml_b_i[l], ml_b_f[l]])
    return {
        "layer": l, "stacks": stacks,
        "norm_mix": norm_mix[l][None, :],
        "w_if": w_if_c, "w_if_t": w_if_r,
        "gm_norm_v": gm_norm_v[l][None, :],
        "gm_w_s": gm_w_s[l],
        "gm_b_s_t": gm_b_s[l].T,
        "lru_conv_w": lru_conv_w[l], "lru_conv_b": lru_conv_b[l][None, :],
        "lru_w_r": lru_w_r[l].astype(BF16), "lru_b_r": lru_b_r[l][None, :],
        "lru_w_i": lru_w_i[l].astype(BF16), "lru_b_i": lru_b_i[l][None, :],
        "lru_lambda": lru_lambda[l][None, :],
        "ml_conv_w_q": ml_conv_w[l][:, :w], "ml_conv_b_q": ml_conv_b[l][None, :w],
        "ml_conv_w_k": ml_conv_w[l][:, w:], "ml_conv_b_k": ml_conv_b[l][None, w:],
        "ml_b_if_c": jnp.pad(b_if, (0, IF_PAD - 2 * N_GROUPS))[None, :],
        "ml_b_if_r": jnp.pad(b_if, (0, IF_ROWS - 2 * N_GROUPS))[:, None],
        "ml_norm_h": ml_norm_h[l].reshape(1, w),
        "sb_norm_q": sb_norm_q[l][None, :], "sb_norm_k": sb_norm_k[l][None, :],
        "norm_ffn": norm_ffn[l][None, :],
    }


def kernel(x, norm_mix, w_in, gm_norm_v, gm_w_s, gm_b_s, lru_conv_w, lru_conv_b, lru_w_r, lru_b_r, lru_w_i, lru_b_i, lru_lambda, ml_conv_w, ml_conv_b, ml_b_i, ml_b_f, ml_norm_h, sb_norm_q, sb_norm_k, w_branch, w_gate, b_gate, w_out, norm_ffn, w_ffn_gate, w_ffn_up, w_ffn_down):
    b, s, d = x.shape
    n_abc = 8 * BRANCH_W
    stacks = {"w_in_abc_bf16": w_in[:, :, :n_abc].astype(BF16),
              "w_in_d_bf16": w_in[:, :, n_abc + 2 * N_GROUPS:].astype(BF16),
              "w_gate": w_gate, "b_gate": b_gate, "w_branch": w_branch, "w_out": w_out,
              "w_ffn_gate": w_ffn_gate, "w_ffn_up": w_ffn_up, "w_ffn_down_bf16": w_ffn_down.astype(BF16)}
    outs = []
    for bi in range(b):
        xb = x.reshape(s, d) if b == 1 else x[bi]
        for l in range(norm_mix.shape[0]):
            p = _prepare_layer(l, stacks, norm_mix, w_in, gm_norm_v, gm_w_s, gm_b_s, lru_conv_w, lru_conv_b,
                               lru_w_r, lru_b_r, lru_w_i, lru_b_i, lru_lambda, ml_conv_w, ml_conv_b, ml_b_i,
                               ml_b_f, ml_norm_h, sb_norm_q, sb_norm_k, w_branch, w_gate, b_gate, w_out,
                               norm_ffn, w_ffn_gate, w_ffn_up, w_ffn_down)
            xb = _layer(xb, p)
        outs.append(xb)
    return outs[0].reshape(b, s, d) if b == 1 else jnp.stack(outs, axis=0)
```

```python
import functools

import jax
import jax.numpy as jnp
from jax import lax
from jax.experimental import pallas as pl
from jax.experimental.pallas import tpu as pltpu

D_MODEL = 2048
N_BRANCH = 4
BRANCH_W = D_MODEL // N_BRANCH
GROUP_DIM = 128
N_GROUPS = BRANCH_W // GROUP_DIM
CHUNK = 128
CONV_W = 4
LRU_C = 8.0
EPS = 1e-6
IF_PAD = 128
IF_ROWS = 16
LOG2E = 1.4426950408889634

F32 = jnp.float32
BF16 = jnp.bfloat16

VMEM_LIMIT_BYTES = 48 * 1024 * 1024

COL_A_U, COL_A_V, COL_B_X, COL_B_G, COL_C_Q, COL_C_K, COL_C_V, COL_C_O = range(8)
COL_D_Q, COL_D_K, COL_D_V = range(3)


def _params(*sem):
    return pltpu.CompilerParams(dimension_semantics=sem, vmem_limit_bytes=VMEM_LIMIT_BYTES)


def _dot(a, b):
    return jnp.dot(a, b, preferred_element_type=F32)


def _dot_nt(a, b):
    return lax.dot_general(a, b, (((1,), (1,)), ((), ())), preferred_element_type=F32)


def _log_sigmoid(x):
    return jnp.minimum(x, 0.0) - jnp.log1p(jnp.exp(-jnp.abs(x)))


def _rms(x):
    return x * lax.rsqrt(jnp.mean(x * x, axis=-1, keepdims=True) + EPS)


def _split3(a):
    a1 = a.astype(BF16)
    r1 = a - a1.astype(F32)
    a2 = r1.astype(BF16)
    a3 = (r1 - a2.astype(F32)).astype(BF16)
    return a1, a2, a3


def _norm_if_kernel(x_ref, g_ref, wif_ref, wift_ref, xn_ref, ifc_ref, ifr_ref):
    xn = (_rms(x_ref[...]) * g_ref[...]).astype(BF16)
    xn_ref[...] = xn
    ifc_ref[...] = _dot(xn, wif_ref[...])
    ifr_ref[...] = _dot_nt(wift_ref[...], xn)


def _norm_kernel(x_ref, g_ref, xn_ref):
    xn_ref[...] = (_rms(x_ref[...]) * g_ref[...]).astype(BF16)


def _norm_if(x, g, wif, wift, tm):
    s, d = x.shape
    return pl.pallas_call(
        _norm_if_kernel,
        grid=(s // tm,),
        in_specs=[pl.BlockSpec((tm, d), lambda i: (i, 0)),
                  pl.BlockSpec((1, d), lambda i: (0, 0)),
                  pl.BlockSpec((d, IF_PAD), lambda i: (0, 0)),
                  pl.BlockSpec((IF_ROWS, d), lambda i: (0, 0))],
        out_specs=[pl.BlockSpec((tm, d), lambda i: (i, 0)),
                   pl.BlockSpec((tm, IF_PAD), lambda i: (i, 0)),
                   pl.BlockSpec((IF_ROWS, tm), lambda i: (0, i))],
        out_shape=[jax.ShapeDtypeStruct((s, d), BF16),
                   jax.ShapeDtypeStruct((s, IF_PAD), F32),
                   jax.ShapeDtypeStruct((IF_ROWS, s), F32)],
        compiler_params=_params("parallel"),
        name="norm_if",
    )(x, g, wif, wift)


def _norm(x, g, tm):
    s, d = x.shape
    return pl.pallas_call(
        _norm_kernel,
        grid=(s // tm,),
        in_specs=[pl.BlockSpec((tm, d), lambda i: (i, 0)),
                  pl.BlockSpec((1, d), lambda i: (0, 0))],
        out_specs=pl.BlockSpec((tm, d), lambda i: (i, 0)),
        out_shape=jax.ShapeDtypeStruct((s, d), BF16),
        compiler_params=_params("parallel"),
        name="norm",
    )(x, g)


def _mm_kernel(x_ref, w_ref, o_ref):
    o_ref[...] = _dot(x_ref[...], w_ref[...]).astype(o_ref.dtype)


def _mm_res_kernel(x_ref, w_ref, r_ref, o_ref):
    o_ref[...] = r_ref[...] + _dot(x_ref[...], w_ref[...])


def _matmul(x, w_stack, layer, out_dtype, tm, tn, name):
    m, k = x.shape
    n = w_stack.shape[2]
    return pl.pallas_call(
        _mm_kernel,
        grid=(m // tm, n // tn),
        in_specs=[pl.BlockSpec((tm, k), lambda i, j: (i, 0)),
                  pl.BlockSpec((None, k, tn), lambda i, j: (layer, 0, j))],
        out_specs=pl.BlockSpec((tm, tn), lambda i, j: (i, j)),
        out_shape=jax.ShapeDtypeStruct((m, n), out_dtype),
        compiler_params=_params("parallel", "parallel"),
        name=name,
    )(x, w_stack)


def _matmul_residual(x, w_stack, layer, res, tm, tn, name):
    m, k = x.shape
    n = w_stack.shape[2]
    return pl.pallas_call(
        _mm_res_kernel,
        grid=(m // tm, n // tn),
        in_specs=[pl.BlockSpec((tm, k), lambda i, j: (i, 0)),
                  pl.BlockSpec((None, k, tn), lambda i, j: (layer, 0, j)),
                  pl.BlockSpec((tm, tn), lambda i, j: (i, j))],
        out_specs=pl.BlockSpec((tm, tn), lambda i, j: (i, j)),
        out_shape=jax.ShapeDtypeStruct((m, n), F32),
        compiler_params=_params("parallel", "parallel"),
        name=name,
    )(x, w_stack, res)


def _out_norm_kernel(x_ref, w_ref, r_ref, g_ref, o_ref, hn_ref):
    xo = r_ref[...] + _dot(x_ref[...], w_ref[...])
    o_ref[...] = xo
    hn_ref[...] = (_rms(xo) * g_ref[...]).astype(BF16)


def _out_proj_norm(merged, w_stack, layer, res, g, tm):
    s, d = merged.shape
    row_spec = pl.BlockSpec((tm, d), lambda i: (i, 0))
    return pl.pallas_call(
        _out_norm_kernel,
        grid=(s // tm,),
        in_specs=[row_spec, pl.BlockSpec((None, d, d), lambda i: (layer, 0, 0)), row_spec,
                  pl.BlockSpec((1, d), lambda i: (0, 0))],
        out_specs=[row_spec, row_spec],
        out_shape=[jax.ShapeDtypeStruct((s, d), F32), jax.ShapeDtypeStruct((s, d), BF16)],
        compiler_params=_params("parallel"),
        name="out_proj_norm",
    )(merged, w_stack, res, g)


def _qkv_kernel(x_ref, w_ref, gq_ref, gk_ref, qo_ref, ko_ref, vo_ref):
    j = pl.program_id(1)
    acc = _dot(x_ref[...], w_ref[...])

    def heads(fn, out_ref):
        for h in range(N_GROUPS):
            out_ref[h] = fn(acc[:, h * GROUP_DIM:(h + 1) * GROUP_DIM]).astype(BF16)

    @pl.when(j == 0)
    def _():
        heads(lambda t: _rms(t) * gq_ref[...] * (GROUP_DIM ** -0.5 * LOG2E), qo_ref)

    @pl.when(j == 1)
    def _():
        heads(lambda t: _rms(t) * gk_ref[...], ko_ref)

    @pl.when(j == 2)
    def _():
        heads(lambda t: t, vo_ref)


def _qkv_proj(xn, w_stack, layer, gq, gk, tm):
    s, d = xn.shape
    hm_spec = pl.BlockSpec((N_GROUPS, tm, GROUP_DIM), lambda i, j: (0, i, 0))
    hm_shape = jax.ShapeDtypeStruct((N_GROUPS, s, GROUP_DIM), BF16)
    g_spec = pl.BlockSpec((1, GROUP_DIM), lambda i, j: (0, 0))
    return pl.pallas_call(
        _qkv_kernel,
        grid=(s // tm, 3),
        in_specs=[pl.BlockSpec((tm, d), lambda i, j: (i, 0)),
                  pl.BlockSpec((None, d, BRANCH_W), lambda i, j: (layer, 0, j)), g_spec, g_spec],
        out_specs=[hm_spec, hm_spec, hm_spec],
        out_shape=[hm_shape, hm_shape, hm_shape],
        compiler_params=_params("arbitrary", "arbitrary"),
        name="qkv_proj",
    )(xn, w_stack, gq, gk)


def _cast_at_first_row_tile(pairs):
    @pl.when(pl.program_id(1) == 0)
    def _():
        for w_ref, wb_s in pairs:
            wb_s[...] = w_ref[...].astype(BF16)


def _mm_ws_kernel(x_ref, w_ref, o_ref, wb_s):
    _cast_at_first_row_tile([(w_ref, wb_s)])
    o_ref[...] = _dot(x_ref[...], wb_s[...]).astype(o_ref.dtype)


def _mm_ws_res_kernel(x_ref, w_ref, r_ref, o_ref, wb_s):
    _cast_at_first_row_tile([(w_ref, wb_s)])
    o_ref[...] = r_ref[...] + _dot(x_ref[...], wb_s[...])


def _matmul_ws(x, w_stack, layer, n, out_dtype, tm, tn, name, res=None):
    m, k = x.shape
    in_specs = [pl.BlockSpec((tm, k), lambda j, i: (i, 0)),
                pl.BlockSpec((None, k, tn), lambda j, i: (layer, 0, j))]
    args = [x, w_stack]
    if res is not None:
        in_specs.append(pl.BlockSpec((tm, tn), lambda j, i: (i, j)))
        args.append(res)
    return pl.pallas_call(
        _mm_ws_kernel if res is None else _mm_ws_res_kernel,
        grid=(n // tn, m // tm),
        in_specs=in_specs,
        out_specs=pl.BlockSpec((tm, tn), lambda j, i: (i, j)),
        out_shape=jax.ShapeDtypeStruct((m, n), out_dtype),
        scratch_shapes=[pltpu.VMEM((k, tn), BF16)],
        compiler_params=_params("arbitrary", "arbitrary"),
        name=name,
    )(*args)


def _merge_kernel(xn_ref, ya_ref, yb_ref, yc_ref, yd_ref, wg_ref, bg_ref, wb_ref, o_ref, wg_s, wb_s):
    _cast_at_first_row_tile([(wg_ref, wg_s), (wb_ref, wb_s)])
    xn = xn_ref[...]
    acc = None
    for g, y_ref in enumerate((ya_ref, yb_ref, yc_ref, yd_ref)):
        gate = jax.nn.sigmoid(_dot(xn, wg_s[g]) + bg_ref[g:g + 1, :])
        term = gate * _dot(y_ref[...], wb_s[g])
        acc = term if acc is None else acc + term
    o_ref[...] = acc.astype(o_ref.dtype)


def _merge(xn, ys, wg_stack, bg_stack, wb_stack, layer, tm, tn):
    s, d = xn.shape
    w = ys[0].shape[1]
    y_spec = pl.BlockSpec((tm, w), lambda j, i: (i, 0))
    return pl.pallas_call(
        _merge_kernel,
        grid=(d // tn, s // tm),
        in_specs=[pl.BlockSpec((tm, d), lambda j, i: (i, 0)),
                  y_spec, y_spec, y_spec, y_spec,
                  pl.BlockSpec((None, N_BRANCH, d, tn), lambda j, i: (layer, 0, 0, j)),
                  pl.BlockSpec((None, N_BRANCH, tn), lambda j, i: (layer, 0, j)),
                  pl.BlockSpec((None, N_BRANCH, w, tn), lambda j, i: (layer, 0, 0, j))],
        out_specs=pl.BlockSpec((tm, tn), lambda j, i: (i, j)),
        out_shape=jax.ShapeDtypeStruct((s, d), BF16),
        scratch_shapes=[pltpu.VMEM((N_BRANCH, d, tn), BF16), pltpu.VMEM((N_BRANCH, w, tn), BF16)],
        compiler_params=_params("arbitrary", "arbitrary"),
        name="merge",
    )(xn, *ys, wg_stack, bg_stack, wb_stack)


def _ffn_up_kernel(x_ref, wg_ref, wu_ref, o_ref, wg_s, wu_s):
    _cast_at_first_row_tile([(wg_ref, wg_s), (wu_ref, wu_s)])
    x = x_ref[...]
    o_ref[...] = (jax.nn.silu(_dot(x, wg_s[...])) * _dot(x, wu_s[...])).astype(o_ref.dtype)


def _ffn_up(hn, wg_stack, wu_stack, layer, tm, tn):
    s, d = hn.shape
    f = wg_stack.shape[2]
    w_spec = pl.BlockSpec((None, d, tn), lambda j, i: (layer, 0, j))
    return pl.pallas_call(
        _ffn_up_kernel,
        grid=(f // tn, s // tm),
        in_specs=[pl.BlockSpec((tm, d), lambda j, i: (i, 0)), w_spec, w_spec],
        out_specs=pl.BlockSpec((tm, tn), lambda j, i: (i, j)),
        out_shape=jax.ShapeDtypeStruct((s, f), BF16),
        scratch_shapes=[pltpu.VMEM((d, tn), BF16), pltpu.VMEM((d, tn), BF16)],
        compiler_params=_params("arbitrary", "arbitrary"),
        name="ffn_up",
    )(hn, wg_stack, wu_stack)


def _col_spec(t, col):
    return pl.BlockSpec((t, BRANCH_W), lambda i: (i, col))


def _halo_spec(t, col):
    return pl.BlockSpec((8, BRANCH_W), lambda i: (jnp.maximum(i * (t // 8) - 1, 0), col))


def _full_spec(shape):
    nd = len(shape)
    return pl.BlockSpec(shape, lambda i: (0,) * nd)


def _shift_rows(x, prev8, j):
    xr = pltpu.roll(x, j, 0)
    pr = pltpu.roll(prev8, j, 0)
    rid = lax.broadcasted_iota(jnp.int32, pr.shape, 0)
    top = jnp.where(rid < j, pr, xr[:8])
    return jnp.concatenate([top, xr[8:]], axis=0)


def _causal_conv(x, prev8, w_ref, b_ref):
    acc = None
    for k in range(CONV_W):
        j = CONV_W - 1 - k
        xs = x if j == 0 else _shift_rows(x, prev8, j)
        term = xs * w_ref[k:k + 1, :]
        acc = term if acc is None else acc + term
    return acc + b_ref[...]


def _gmlp_kernel(u_ref, v_ref, gv_ref, ws_ref, bst_ref, o_ref):
    t = u_ref.shape[0]
    u = jax.nn.gelu(u_ref[...])
    v = (_rms(jax.nn.gelu(v_ref[...])) * gv_ref[...]).astype(BF16)
    row = lax.broadcasted_iota(jnp.int32, (CHUNK, CHUNK), 0)
    col = lax.broadcasted_iota(jnp.int32, (CHUNK, CHUNK), 1)
    for g in range(N_GROUPS):
        gs = slice(g * GROUP_DIM, (g + 1) * GROUP_DIM)
        ws = jnp.where(col <= row, ws_ref[g], 0.0).astype(BF16)
        bcol = bst_ref[:, g:g + 1]
        for c in range(t // CHUNK):
            cs = slice(c * CHUNK, (c + 1) * CHUNK)
            sp = _dot(ws, v[cs, gs]) + bcol
            o_ref[cs, gs] = (u[cs, gs] * sp).astype(o_ref.dtype)


def _gmlp(proj, gv, ws, bst, t):
    s = proj.shape[0]
    return pl.pallas_call(
        _gmlp_kernel,
        grid=(s // t,),
        in_specs=[_col_spec(t, COL_A_U), _col_spec(t, COL_A_V),
                  _full_spec((1, BRANCH_W)), _full_spec((N_GROUPS, CHUNK, CHUNK)),
                  _full_spec((CHUNK, N_GROUPS))],
        out_specs=pl.BlockSpec((t, BRANCH_W), lambda i: (i, 0)),
        out_shape=jax.ShapeDtypeStruct((s, BRANCH_W), BF16),
        compiler_params=_params("parallel"),
        name="gmlp",
    )(proj, proj, gv, ws, bst)


def _lin_scan(a, b):
    t = a.shape[0]
    rid = lax.broadcasted_iota(jnp.int32, a.shape, 0)
    d = 1
    while d < t:
        keep = rid >= d
        a_sh = jnp.where(keep, pltpu.roll(a, d, 0), 1.0)
        b_sh = jnp.where(keep, pltpu.roll(b, d, 0), 0.0)
        b = a * b_sh + b
        a = a * a_sh
        d *= 2
    return a, b


def _rglru_kernel(x_ref, xp_ref, gt_ref, cw_ref, cb_ref, wr_ref, br_ref, wi_ref, bi_ref, lam_ref,
                  o_ref, h_s):
    i = pl.program_id(0)

    @pl.when(i == 0)
    def _():
        h_s[...] = jnp.zeros_like(h_s)

    prev = jnp.where(i == 0, 0.0, xp_ref[...])
    xc = _causal_conv(x_ref[...], prev, cw_ref, cb_ref)
    xcb = xc.astype(BF16)
    r_parts, i_parts = [], []
    for g in range(N_GROUPS):
        gs = slice(g * GROUP_DIM, (g + 1) * GROUP_DIM)
        r_parts.append(_dot(xcb[:, gs], wr_ref[g]))
        i_parts.append(_dot(xcb[:, gs], wi_ref[g]))
    r = jax.nn.sigmoid(jnp.concatenate(r_parts, axis=1) + br_ref[...])
    ig = jax.nn.sigmoid(jnp.concatenate(i_parts, axis=1) + bi_ref[...])
    log_a = LRU_C * r * _log_sigmoid(lam_ref[...])
    a = jnp.exp(log_a)
    bx = jnp.sqrt(-jnp.tanh(log_a) * (a * a + 1.0)) * (ig * xc)
    a_cum, h_loc = _lin_scan(a, bx)
    h = h_loc + a_cum * h_s[0:1, :]
    t = h.shape[0]
    h_s[...] = jnp.broadcast_to(h[t - 1:t, :], h_s.shape)
    o_ref[...] = (h * jax.nn.gelu(gt_ref[...])).astype(o_ref.dtype)


def _rglru(proj, cw, cb, wr, br, wi, bi, lam, t):
    s = proj.shape[0]
    return pl.pallas_call(
        _rglru_kernel,
        grid=(s // t,),
        in_specs=[_col_spec(t, COL_B_X), _halo_spec(t, COL_B_X), _col_spec(t, COL_B_G),
                  _full_spec((CONV_W, BRANCH_W)), _full_spec((1, BRANCH_W)),
                  _full_spec((N_GROUPS, GROUP_DIM, GROUP_DIM)), _full_spec((1, BRANCH_W)),
                  _full_spec((N_GROUPS, GROUP_DIM, GROUP_DIM)), _full_spec((1, BRANCH_W)),
                  _full_spec((1, BRANCH_W))],
        out_specs=pl.BlockSpec((t, BRANCH_W), lambda i: (i, 0)),
        out_shape=jax.ShapeDtypeStruct((s, BRANCH_W), BF16),
        scratch_shapes=[pltpu.VMEM((8, BRANCH_W), F32)],
        compiler_params=_params("arbitrary"),
        name="rglru",
    )(proj, proj, proj, cw, cb, wr, br, wi, bi, lam)


def _mlstm_kernel(q_ref, qp_ref, k_ref, kp_ref, v_ref, og_ref, ifc_ref, ifr_ref,
                  cwq_ref, cbq_ref, cwk_ref, cbk_ref, bifc_ref, bifr_ref, gh_ref,
                  y_ref, ct_s, n_s, m_s):
    i = pl.program_id(0)

    @pl.when(i == 0)
    def _():
        ct_s[...] = jnp.zeros_like(ct_s)
        n_s[...] = jnp.zeros_like(n_s)
        m_s[...] = jnp.zeros_like(m_s)

    first = i == 0
    q = jax.nn.silu(_causal_conv(q_ref[...], jnp.where(first, 0.0, qp_ref[...]), cwq_ref, cbq_ref))
    k = jax.nn.silu(_causal_conv(k_ref[...], jnp.where(first, 0.0, kp_ref[...]), cwk_ref, cbk_ref))
    k = k * (GROUP_DIM ** -0.5)
    v = v_ref[...]
    og = og_ref[...]

    row = lax.broadcasted_iota(jnp.int32, (CHUNK, CHUNK), 0)
    col = lax.broadcasted_iota(jnp.int32, (CHUNK, CHUNK), 1)
    tril = col <= row
    tril_b = jnp.where(tril, 1.0, 0.0).astype(BF16)
    triu_b = jnp.where(row <= col, 1.0, 0.0).astype(BF16)
    state = [(ct_s[h], n_s[h:h + 1, :], m_s[h:h + 1, 0:1]) for h in range(N_GROUPS)]

    for c in range(q.shape[0] // CHUNK):
        cs = slice(c * CHUNK, (c + 1) * CHUNK)
        ifc = ifc_ref[cs, :] + bifc_ref[...]
        ifr = ifr_ref[:, cs] + bifr_ref[...]
        bcc = sum(_dot(tril_b, piece) for piece in _split3(_log_sigmoid(ifc)))
        bcr = sum(_dot(piece, triu_b) for piece in _split3(_log_sigmoid(ifr)))

        for h in range(N_GROUPS):
            hs = slice(h * GROUP_DIM, (h + 1) * GROUP_DIM)
            qh, kh, vh = q[cs, hs], k[cs, hs], v[cs, hs]
            ic_row, ic_col = ifr[h:h + 1, :], ifc[:, h:h + 1]
            bc_row, bc_col = bcr[N_GROUPS + h:N_GROUPS + h + 1, :], bcc[:, N_GROUPS + h:N_GROUPS + h + 1]
            ct, n_row, m_st = state[h]

            dmat = jnp.where(tril, bc_col - bc_row + ic_row, -jnp.inf)
            m_inter = bc_col + m_st
            m_t = jnp.maximum(m_inter, jnp.max(dmat, axis=-1, keepdims=True))
            kt = kh.T
            qb, vb = qh.astype(BF16), vh.astype(BF16)
            p = _dot(qb, kt.astype(BF16)) * jnp.exp(dmat - m_t)
            sc = jnp.exp(m_inter - m_t)
            num = _dot(p.astype(BF16), vb) + sc * _dot(qb, ct.astype(BF16))
            den = jnp.sum(p, axis=-1, keepdims=True) + sc * jnp.sum(qh * n_row, axis=-1, keepdims=True)
            hh = num / jnp.maximum(jnp.abs(den), jnp.exp(-m_t))

            b_last = bc_row[:, CHUNK - 1:CHUNK]
            g_row = b_last - bc_row + ic_row
            g_col = b_last - bc_col + ic_col
            m_new = jnp.maximum(b_last + m_st, jnp.max(g_row, axis=-1, keepdims=True))
            w_row = jnp.exp(g_row - m_new)
            w_col = jnp.exp(g_col - m_new)
            decay = jnp.exp(b_last + m_st - m_new)
            state[h] = (decay * ct + _dot((kt * w_row).astype(BF16), vb),
                        decay * n_row + jnp.sum(kh * w_col, axis=0, keepdims=True),
                        m_new)

            hn = _rms(hh) * gh_ref[:, hs]
            y_ref[cs, hs] = (jax.nn.sigmoid(og[cs, hs]) * hn).astype(y_ref.dtype)

    for h in range(N_GROUPS):
        ct_s[h] = state[h][0]
        n_s[h:h + 1, :] = state[h][1]
        m_s[h:h + 1, :] = jnp.broadcast_to(state[h][2], (1, GROUP_DIM))


def _mlstm(proj, ifc, ifr, cwq, cbq, cwk, cbk, bifc, bifr, gh, t):
    s = proj.shape[0]
    return pl.pallas_call(
        _mlstm_kernel,
        grid=(s // t,),
        in_specs=[_col_spec(t, COL_C_Q), _halo_spec(t, COL_C_Q),
                  _col_spec(t, COL_C_K), _halo_spec(t, COL_C_K),
                  _col_spec(t, COL_C_V), _col_spec(t, COL_C_O),
                  pl.BlockSpec((t, IF_PAD), lambda i: (i, 0)),
                  pl.BlockSpec((IF_ROWS, t), lambda i: (0, i)),
                  _full_spec((CONV_W, BRANCH_W)), _full_spec((1, BRANCH_W)),
                  _full_spec((CONV_W, BRANCH_W)), _full_spec((1, BRANCH_W)),
                  _full_spec((1, IF_PAD)), _full_spec((IF_ROWS, 1)),
                  _full_spec((1, BRANCH_W))],
        out_specs=pl.BlockSpec((t, BRANCH_W), lambda i: (i, 0)),
        out_shape=jax.ShapeDtypeStruct((s, BRANCH_W), BF16),
        scratch_shapes=[pltpu.VMEM((N_GROUPS, GROUP_DIM, GROUP_DIM), F32),
                        pltpu.VMEM((8, GROUP_DIM), F32),
                        pltpu.VMEM((8, GROUP_DIM), F32)],
        compiler_params=_params("arbitrary"),
        name="mlstm",
    )(proj, proj, proj, proj, proj, proj, ifc, ifr, cwq, cbq, cwk, cbk, bifc, bifr, gh)


def _sb_prep_kernel(q_ref, k_ref, v_ref, gq_ref, gk_ref, qo_ref, ko_ref, vo_ref):
    q, k, v = q_ref[...], k_ref[...], v_ref[...]
    for h in range(N_GROUPS):
        hs = slice(h * GROUP_DIM, (h + 1) * GROUP_DIM)
        qo_ref[h] = (_rms(q[:, hs]) * gq_ref[...] * (GROUP_DIM ** -0.5 * LOG2E)).astype(BF16)
        ko_ref[h] = (_rms(k[:, hs]) * gk_ref[...]).astype(BF16)
        vo_ref[h] = v[:, hs].astype(BF16)


def _sb_prep(proj, gq, gk, t):
    s = proj.shape[0]
    hm_spec = pl.BlockSpec((N_GROUPS, t, GROUP_DIM), lambda i: (0, i, 0))
    hm_shape = jax.ShapeDtypeStruct((N_GROUPS, s, GROUP_DIM), BF16)
    return pl.pallas_call(
        _sb_prep_kernel,
        grid=(s // t,),
        in_specs=[_col_spec(t, COL_D_Q), _col_spec(t, COL_D_K), _col_spec(t, COL_D_V),
                  _full_spec((1, GROUP_DIM)), _full_spec((1, GROUP_DIM))],
        out_specs=[hm_spec, hm_spec, hm_spec],
        out_shape=[hm_shape, hm_shape, hm_shape],
        compiler_params=_params("parallel"),
        name="sb_prep",
    )(proj, proj, proj, gq, gk)


def _sb_kernel(q_ref, k_ref, v_ref, o_ref, *, tq, sub):
    qi = pl.program_id(1)
    nsub = tq // sub
    q = q_ref[0]
    krow = lax.broadcasted_iota(jnp.int32, (sub, sub), 0)
    kcol = lax.broadcasted_iota(jnp.int32, (sub, sub), 1)
    from_j = jnp.where(krow >= kcol, 1.0, 0.0).astype(BF16)

    def piece(qa, start, strict, run):
        kb = k_ref[0, pl.ds(start, sub), :]
        z = _dot_nt(qa, kb)
        neg_abs = lax.bitcast_convert_type(
            lax.bitcast_convert_type(z, jnp.uint32) | jnp.uint32(0x80000000), F32)
        sp = jnp.maximum(z, 0.0) + jnp.log(1.0 + jnp.exp2(neg_abs)) * LOG2E
        if strict is not None:
            sp = jnp.where(strict, sp, 0.0)
        att = jnp.exp2(z - _dot(sp.astype(BF16), from_j) - run)
        if strict is not None:
            att = jnp.where(strict, att, 0.0)
        return att.astype(BF16), run + jnp.sum(sp, axis=-1, keepdims=True)

    acc = jnp.zeros((tq, GROUP_DIM), F32)
    run = jnp.zeros((tq, 1), F32)
    for b in range(nsub - 1, -1, -1):
        r0 = b * sub
        qpos = lax.broadcasted_iota(jnp.int32, (tq - r0, sub), 0)
        kpos = lax.broadcasted_iota(jnp.int32, (tq - r0, sub), 1)
        start = pl.multiple_of(qi * tq + r0, sub)
        att, run_b = piece(q[r0:], start, kpos < qpos, run[r0:])
        acc_b = acc[r0:] + _dot(att, v_ref[0, pl.ds(start, sub), :])
        acc = acc_b if b == 0 else jnp.concatenate([acc[:r0], acc_b], axis=0)
        run = run_b if b == 0 else jnp.concatenate([run[:r0], run_b], axis=0)

    def tile(j, carry):
        acc, run = carry
        base = (qi - 1 - j) * tq
        atts = []
        for b in range(nsub - 1, -1, -1):
            att, run = piece(q, pl.multiple_of(base + b * sub, sub), None, run)
            atts.append(att)
        vb = v_ref[0, pl.ds(pl.multiple_of(base, tq), tq), :]
        return acc + _dot(jnp.concatenate(atts[::-1], axis=1), vb), run

    acc, run = lax.fori_loop(0, qi, tile, (acc, run))
    o_ref[...] = acc.astype(o_ref.dtype)


def _sb_attention(qh, kh, vh, tq):
    _, s, _ = qh.shape
    return pl.pallas_call(
        functools.partial(_sb_kernel, tq=tq, sub=min(tq, 256)),
        grid=(N_GROUPS, s // tq),
        in_specs=[pl.BlockSpec((1, tq, GROUP_DIM), lambda h, i: (h, i, 0)),
                  pl.BlockSpec((1, s, GROUP_DIM), lambda h, i: (h, 0, 0)),
                  pl.BlockSpec((1, s, GROUP_DIM), lambda h, i: (h, 0, 0))],
        out_specs=pl.BlockSpec((tq, GROUP_DIM), lambda h, i: (i, h)),
        out_shape=jax.ShapeDtypeStruct((s, BRANCH_W), BF16),
        compiler_params=_params("parallel", "parallel"),
        name="sb_attn",
    )(qh, kh, vh)


def _tile(s, pref):
    return min(s, pref)


def _layer(x, p):
    s = x.shape[0]
    l, st = p["layer"], p["stacks"]
    xn, ifc, ifr = _norm_if(x, p["norm_mix"], p["w_if"], p["w_if_t"], _tile(s, 512))
    proj = _matmul(xn, st["w_in_abc_bf16"], l, F32, _tile(s, 1024), 512, "in_proj")
    qh, kh, vh = _qkv_proj(xn, st["w_in_d_bf16"], l, p["sb_norm_q"], p["sb_norm_k"], _tile(s, 1024))

    y_a = _gmlp(proj, p["gm_norm_v"], p["gm_w_s"], p["gm_b_s_t"], _tile(s, 512))
    y_b = _rglru(proj, p["lru_conv_w"], p["lru_conv_b"], p["lru_w_r"], p["lru_b_r"],
                 p["lru_w_i"], p["lru_b_i"], p["lru_lambda"], _tile(s, 512))
    y_c = _mlstm(proj, ifc, ifr, p["ml_conv_w_q"], p["ml_conv_b_q"], p["ml_conv_w_k"], p["ml_conv_b_k"],
                 p["ml_b_if_c"], p["ml_b_if_r"], p["ml_norm_h"], CHUNK)
    y_d = _sb_attention(qh, kh, vh, _tile(s, 1024))

    merged = _merge(xn, (y_a, y_b, y_c, y_d), st["w_gate"], st["b_gate"], st["w_branch"], l, _tile(s, 1024), 256)
    x, hn = _out_proj_norm(merged, st["w_out_bf16"], l, x, p["norm_ffn"], _tile(s, 512))
    hid = _ffn_up(hn, st["w_ffn_gate"], st["w_ffn_up"], l, _tile(s, 1024), 512)
    return _matmul_residual(hid, st["w_ffn_down_bf16"], l, x, _tile(s, 512), 512, "ffn_down")


def _prepare_layer(l, stacks, norm_mix, w_in, gm_norm_v, gm_w_s, gm_b_s, lru_conv_w, lru_conv_b, lru_w_r, lru_b_r,
                   lru_w_i, lru_b_i, lru_lambda, ml_conv_w, ml_conv_b, ml_b_i, ml_b_f, ml_norm_h, sb_norm_q,
                   sb_norm_k, w_branch, w_gate, b_gate, w_out, norm_ffn, w_ffn_gate, w_ffn_up, w_ffn_down):
    w = BRANCH_W
    n_main = 8 * w
    w_if = w_in[l, :, n_main:n_main + 2 * N_GROUPS].astype(BF16)
    w_if_c = jnp.pad(w_if, ((0, 0), (0, IF_PAD - 2 * N_GROUPS)))
    w_if_r = jnp.pad(w_if.T, ((0, IF_ROWS - 2 * N_GROUPS), (0, 0)))
    b_if = jnp.concatenate([ml_b_i[l], ml_b_f[l]])
    return {
        "layer": l, "stacks": stacks,
        "norm_mix": norm_mix[l][None, :],
        "w_if": w_if_c, "w_if_t": w_if_r,
        "gm_norm_v": gm_norm_v[l][None, :],
        "gm_w_s": gm_w_s[l],
        "gm_b_s_t": gm_b_s[l].T,
        "lru_conv_w": lru_conv_w[l], "lru_conv_b": lru_conv_b[l][None, :],
        "lru_w_r": lru_w_r[l].astype(BF16), "lru_b_r": lru_b_r[l][None, :],
        "lru_w_i": lru_w_i[l].astype(BF16), "lru_b_i": lru_b_i[l][None, :],
        "lru_lambda": lru_lambda[l][None, :],
        "ml_conv_w_q": ml_conv_w[l][:, :w], "ml_conv_b_q": ml_conv_b[l][None, :w],
        "ml_conv_w_k": ml_conv_w[l][:, w:], "ml_conv_b_k": ml_conv_b[l][None, w:],
        "ml_b_if_c": jnp.pad(b_if, (0, IF_PAD - 2 * N_GROUPS))[None, :],
        "ml_b_if_r": jnp.pad(b_if, (0, IF_ROWS - 2 * N_GROUPS))[:, None],
        "ml_norm_h": ml_norm_h[l].reshape(1, w),
        "sb_norm_q": sb_norm_q[l][None, :], "sb_norm_k": sb_norm_k[l][None, :],
        "norm_ffn": norm_ffn[l][None, :],
    }


def kernel(x, norm_mix, w_in, gm_norm_v, gm_w_s, gm_b_s, lru_conv_w, lru_conv_b, lru_w_r, lru_b_r, lru_w_i, lru_b_i, lru_lambda, ml_conv_w, ml_conv_b, ml_b_i, ml_b_f, ml_norm_h, sb_norm_q, sb_norm_k, w_branch, w_gate, b_gate, w_out, norm_ffn, w_ffn_gate, w_ffn_up, w_ffn_down):
    b, s, d = x.shape
    n_abc = 8 * BRANCH_W
    stacks = {"w_in_abc_bf16": w_in[:, :, :n_abc].astype(BF16),
              "w_in_d_bf16": w_in[:, :, n_abc + 2 * N_GROUPS:].astype(BF16),
              "w_gate": w_gate, "b_gate": b_gate, "w_branch": w_branch, "w_out_bf16": w_out.astype(BF16),
              "w_ffn_gate": w_ffn_gate, "w_ffn_up": w_ffn_up, "w_ffn_down_bf16": w_ffn_down.astype(BF16)}
    outs = []
    for bi in range(b):
        xb = x.reshape(s, d) if b == 1 else x[bi]
        for l in range(norm_mix.shape[0]):
            p = _prepare_layer(l, stacks, norm_mix, w_in, gm_norm_v, gm_w_s, gm_b_s, lru_conv_w, lru_conv_b,
                               lru_w_r, lru_b_r, lru_w_i, lru_b_i, lru_lambda, ml_conv_w, ml_conv_b, ml_b_i,
                               ml_b_f, ml_norm_h, sb_norm_q, sb_norm_k, w_branch, w_gate, b_gate, w_out,
                               norm_ffn, w_ffn_gate, w_ffn_up, w_ffn_down)
            xb = _layer(xb, p)
        outs.append(xb)
    return outs[0].reshape(b, s, d) if b == 1 else jnp.stack(outs, axis=0)
```

```python
import functools

import jax
import jax.numpy as jnp
from jax import lax
from jax.experimental import pallas as pl
from jax.experimental.pallas import tpu as pltpu

D_MODEL = 2048
N_BRANCH = 4
BRANCH_W = D_MODEL // N_BRANCH
GROUP_DIM = 128
N_GROUPS = BRANCH_W // GROUP_DIM
CHUNK = 128
CONV_W = 4
LRU_C = 8.0
EPS = 1e-6
IF_PAD = 128
IF_ROWS = 16
LOG2E = 1.4426950408889634

F32 = jnp.float32
BF16 = jnp.bfloat16

VMEM_LIMIT_BYTES = 48 * 1024 * 1024

COL_A_U, COL_A_V, COL_B_X, COL_B_G, COL_C_Q, COL_C_K, COL_C_V, COL_C_O = range(8)


def _params(*sem):
    return pltpu.CompilerParams(dimension_semantics=sem, vmem_limit_bytes=VMEM_LIMIT_BYTES)


def _dot(a, b):
    return jnp.dot(a, b, preferred_element_type=F32)


def _dot_nt(a, b):
    return lax.dot_general(a, b, (((1,), (1,)), ((), ())), preferred_element_type=F32)


def _log_sigmoid(x):
    return jnp.minimum(x, 0.0) - jnp.log1p(jnp.exp(-jnp.abs(x)))


def _rms(x):
    return x * lax.rsqrt(jnp.mean(x * x, axis=-1, keepdims=True) + EPS)


def _split3(a):
    a1 = a.astype(BF16)
    r1 = a - a1.astype(F32)
    a2 = r1.astype(BF16)
    a3 = (r1 - a2.astype(F32)).astype(BF16)
    return a1, a2, a3


def _norm_if_kernel(x_ref, g_ref, wif_ref, wift_ref, xn_ref, ifc_ref, ifr_ref):
    xn = (_rms(x_ref[...]) * g_ref[...]).astype(BF16)
    xn_ref[...] = xn
    ifc_ref[...] = _dot(xn, wif_ref[...])
    ifr_ref[...] = _dot_nt(wift_ref[...], xn)


def _norm_if(x, g, wif, wift, tm):
    s, d = x.shape
    return pl.pallas_call(
        _norm_if_kernel,
        grid=(s // tm,),
        in_specs=[pl.BlockSpec((tm, d), lambda i: (i, 0)),
                  pl.BlockSpec((1, d), lambda i: (0, 0)),
                  pl.BlockSpec((d, IF_PAD), lambda i: (0, 0)),
                  pl.BlockSpec((IF_ROWS, d), lambda i: (0, 0))],
        out_specs=[pl.BlockSpec((tm, d), lambda i: (i, 0)),
                   pl.BlockSpec((tm, IF_PAD), lambda i: (i, 0)),
                   pl.BlockSpec((IF_ROWS, tm), lambda i: (0, i))],
        out_shape=[jax.ShapeDtypeStruct((s, d), BF16),
                   jax.ShapeDtypeStruct((s, IF_PAD), F32),
                   jax.ShapeDtypeStruct((IF_ROWS, s), F32)],
        compiler_params=_params("parallel"),
        name="norm_if",
    )(x, g, wif, wift)


def _mm_kernel(x_ref, w_ref, o_ref):
    o_ref[...] = _dot(x_ref[...], w_ref[...]).astype(o_ref.dtype)


def _mm_res_kernel(x_ref, w_ref, r_ref, o_ref):
    o_ref[...] = r_ref[...] + _dot(x_ref[...], w_ref[...])


def _matmul(x, w_stack, layer, out_dtype, tm, tn, name):
    m, k = x.shape
    n = w_stack.shape[2]
    return pl.pallas_call(
        _mm_kernel,
        grid=(m // tm, n // tn),
        in_specs=[pl.BlockSpec((tm, k), lambda i, j: (i, 0)),
                  pl.BlockSpec((None, k, tn), lambda i, j: (layer, 0, j))],
        out_specs=pl.BlockSpec((tm, tn), lambda i, j: (i, j)),
        out_shape=jax.ShapeDtypeStruct((m, n), out_dtype),
        compiler_params=_params("parallel", "parallel"),
        name=name,
    )(x, w_stack)


def _matmul_residual(x, w_stack, layer, res, tm, tn, name):
    m, k = x.shape
    n = w_stack.shape[2]
    return pl.pallas_call(
        _mm_res_kernel,
        grid=(m // tm, n // tn),
        in_specs=[pl.BlockSpec((tm, k), lambda i, j: (i, 0)),
                  pl.BlockSpec((None, k, tn), lambda i, j: (layer, 0, j)),
                  pl.BlockSpec((tm, tn), lambda i, j: (i, j))],
        out_specs=pl.BlockSpec((tm, tn), lambda i, j: (i, j)),
        out_shape=jax.ShapeDtypeStruct((m, n), F32),
        compiler_params=_params("parallel", "parallel"),
        name=name,
    )(x, w_stack, res)


def _out_norm_kernel(x_ref, w_ref, r_ref, g_ref, o_ref, hn_ref):
    xo = r_ref[...] + _dot(x_ref[...], w_ref[...])
    o_ref[...] = xo
    hn_ref[...] = (_rms(xo) * g_ref[...]).astype(BF16)


def _out_proj_norm(merged, w_stack, layer, res, g, tm):
    s, d = merged.shape
    row_spec = pl.BlockSpec((tm, d), lambda i: (i, 0))
    return pl.pallas_call(
        _out_norm_kernel,
        grid=(s // tm,),
        in_specs=[row_spec, pl.BlockSpec((None, d, d), lambda i: (layer, 0, 0)), row_spec,
                  pl.BlockSpec((1, d), lambda i: (0, 0))],
        out_specs=[row_spec, row_spec],
        out_shape=[jax.ShapeDtypeStruct((s, d), F32), jax.ShapeDtypeStruct((s, d), BF16)],
        compiler_params=_params("parallel"),
        name="out_proj_norm",
    )(merged, w_stack, res, g)


def _qkv_kernel(x_ref, w_ref, gq_ref, gk_ref, qo_ref, ko_ref, vo_ref):
    j = pl.program_id(1)
    acc = _dot(x_ref[...], w_ref[...])

    def heads(fn, out_ref):
        for h in range(N_GROUPS):
            out_ref[h] = fn(acc[:, h * GROUP_DIM:(h + 1) * GROUP_DIM]).astype(BF16)

    @pl.when(j == 0)
    def _():
        heads(lambda t: _rms(t) * gq_ref[...] * (GROUP_DIM ** -0.5 * LOG2E), qo_ref)

    @pl.when(j == 1)
    def _():
        heads(lambda t: _rms(t) * gk_ref[...], ko_ref)

    @pl.when(j == 2)
    def _():
        heads(lambda t: t, vo_ref)


def _qkv_proj(xn, w_stack, layer, gq, gk, tm):
    s, d = xn.shape
    hm_spec = pl.BlockSpec((N_GROUPS, tm, GROUP_DIM), lambda i, j: (0, i, 0))
    hm_shape = jax.ShapeDtypeStruct((N_GROUPS, s, GROUP_DIM), BF16)
    g_spec = pl.BlockSpec((1, GROUP_DIM), lambda i, j: (0, 0))
    return pl.pallas_call(
        _qkv_kernel,
        grid=(s // tm, 3),
        in_specs=[pl.BlockSpec((tm, d), lambda i, j: (i, 0)),
                  pl.BlockSpec((None, d, BRANCH_W), lambda i, j: (layer, 0, j)), g_spec, g_spec],
        out_specs=[hm_spec, hm_spec, hm_spec],
        out_shape=[hm_shape, hm_shape, hm_shape],
        compiler_params=_params("arbitrary", "arbitrary"),
        name="qkv_proj",
    )(xn, w_stack, gq, gk)


def _cast_at_first_row_tile(pairs):
    @pl.when(pl.program_id(1) == 0)
    def _():
        for w_ref, wb_s in pairs:
            wb_s[...] = w_ref[...].astype(BF16)


def _merge_kernel(xn_ref, ya_ref, yb_ref, yc_ref, yd_ref, wg_ref, bg_ref, wb_ref, o_ref, wg_s, wb_s):
    _cast_at_first_row_tile([(wg_ref, wg_s), (wb_ref, wb_s)])
    xn = xn_ref[...]
    acc = None
    for g, y_ref in enumerate((ya_ref, yb_ref, yc_ref, yd_ref)):
        gate = jax.nn.sigmoid(_dot(xn, wg_s[g]) + bg_ref[g:g + 1, :])
        term = gate * _dot(y_ref[...], wb_s[g])
        acc = term if acc is None else acc + term
    o_ref[...] = acc.astype(o_ref.dtype)


def _merge(xn, ys, wg_stack, bg_stack, wb_stack, layer, tm, tn):
    s, d = xn.shape
    w = ys[0].shape[1]
    y_spec = pl.BlockSpec((tm, w), lambda j, i: (i, 0))
    return pl.pallas_call(
        _merge_kernel,
        grid=(d // tn, s // tm),
        in_specs=[pl.BlockSpec((tm, d), lambda j, i: (i, 0)),
                  y_spec, y_spec, y_spec, y_spec,
                  pl.BlockSpec((None, N_BRANCH, d, tn), lambda j, i: (layer, 0, 0, j)),
                  pl.BlockSpec((None, N_BRANCH, tn), lambda j, i: (layer, 0, j)),
                  pl.BlockSpec((None, N_BRANCH, w, tn), lambda j, i: (layer, 0, 0, j))],
        out_specs=pl.BlockSpec((tm, tn), lambda j, i: (i, j)),
        out_shape=jax.ShapeDtypeStruct((s, d), BF16),
        scratch_shapes=[pltpu.VMEM((N_BRANCH, d, tn), BF16), pltpu.VMEM((N_BRANCH, w, tn), BF16)],
        compiler_params=_params("arbitrary", "arbitrary"),
        name="merge",
    )(xn, *ys, wg_stack, bg_stack, wb_stack)


def _ffn_up_kernel(x_ref, wg_ref, wu_ref, o_ref, wg_s, wu_s):
    _cast_at_first_row_tile([(wg_ref, wg_s), (wu_ref, wu_s)])
    x = x_ref[...]
    o_ref[...] = (jax.nn.silu(_dot(x, wg_s[...])) * _dot(x, wu_s[...])).astype(o_ref.dtype)


def _ffn_up(hn, wg_stack, wu_stack, layer, tm, tn):
    s, d = hn.shape
    f = wg_stack.shape[2]
    w_spec = pl.BlockSpec((None, d, tn), lambda j, i: (layer, 0, j))
    return pl.pallas_call(
        _ffn_up_kernel,
        grid=(f // tn, s // tm),
        in_specs=[pl.BlockSpec((tm, d), lambda j, i: (i, 0)), w_spec, w_spec],
        out_specs=pl.BlockSpec((tm, tn), lambda j, i: (i, j)),
        out_shape=jax.ShapeDtypeStruct((s, f), BF16),
        scratch_shapes=[pltpu.VMEM((d, tn), BF16), pltpu.VMEM((d, tn), BF16)],
        compiler_params=_params("arbitrary", "arbitrary"),
        name="ffn_up",
    )(hn, wg_stack, wu_stack)


def _col_spec(t, col):
    return pl.BlockSpec((t, BRANCH_W), lambda i: (i, col))


def _halo_spec(t, col):
    return pl.BlockSpec((8, BRANCH_W), lambda i: (jnp.maximum(i * (t // 8) - 1, 0), col))


def _full_spec(shape):
    nd = len(shape)
    return pl.BlockSpec(shape, lambda i: (0,) * nd)


def _shift_rows(x, prev8, j):
    xr = pltpu.roll(x, j, 0)
    pr = pltpu.roll(prev8, j, 0)
    rid = lax.broadcasted_iota(jnp.int32, pr.shape, 0)
    top = jnp.where(rid < j, pr, xr[:8])
    return jnp.concatenate([top, xr[8:]], axis=0)


def _causal_conv(x, prev8, w_ref, b_ref):
    acc = None
    for k in range(CONV_W):
        j = CONV_W - 1 - k
        xs = x if j == 0 else _shift_rows(x, prev8, j)
        term = xs * w_ref[k:k + 1, :]
        acc = term if acc is None else acc + term
    return acc + b_ref[...]


def _gmlp_kernel(u_ref, v_ref, gv_ref, ws_ref, bst_ref, o_ref):
    t = u_ref.shape[0]
    u = jax.nn.gelu(u_ref[...])
    v = (_rms(jax.nn.gelu(v_ref[...])) * gv_ref[...]).astype(BF16)
    row = lax.broadcasted_iota(jnp.int32, (CHUNK, CHUNK), 0)
    col = lax.broadcasted_iota(jnp.int32, (CHUNK, CHUNK), 1)
    for g in range(N_GROUPS):
        gs = slice(g * GROUP_DIM, (g + 1) * GROUP_DIM)
        ws = jnp.where(col <= row, ws_ref[g], 0.0).astype(BF16)
        bcol = bst_ref[:, g:g + 1]
        for c in range(t // CHUNK):
            cs = slice(c * CHUNK, (c + 1) * CHUNK)
            sp = _dot(ws, v[cs, gs]) + bcol
            o_ref[cs, gs] = (u[cs, gs] * sp).astype(o_ref.dtype)


def _gmlp(proj, gv, ws, bst, t):
    s = proj.shape[0]
    return pl.pallas_call(
        _gmlp_kernel,
        grid=(s // t,),
        in_specs=[_col_spec(t, COL_A_U), _col_spec(t, COL_A_V),
                  _full_spec((1, BRANCH_W)), _full_spec((N_GROUPS, CHUNK, CHUNK)),
                  _full_spec((CHUNK, N_GROUPS))],
        out_specs=pl.BlockSpec((t, BRANCH_W), lambda i: (i, 0)),
        out_shape=jax.ShapeDtypeStruct((s, BRANCH_W), BF16),
        compiler_params=_params("parallel"),
        name="gmlp",
    )(proj, proj, gv, ws, bst)


def _lin_scan(a, b):
    t = a.shape[0]
    rid = lax.broadcasted_iota(jnp.int32, a.shape, 0)
    d = 1
    while d < t:
        keep = rid >= d
        a_sh = jnp.where(keep, pltpu.roll(a, d, 0), 1.0)
        b_sh = jnp.where(keep, pltpu.roll(b, d, 0), 0.0)
        b = a * b_sh + b
        a = a * a_sh
        d *= 2
    return a, b


def _rglru_kernel(x_ref, xp_ref, gt_ref, cw_ref, cb_ref, wr_ref, br_ref, wi_ref, bi_ref, lam_ref,
                  o_ref, h_s):
    i = pl.program_id(0)

    @pl.when(i == 0)
    def _():
        h_s[...] = jnp.zeros_like(h_s)

    prev = jnp.where(i == 0, 0.0, xp_ref[...])
    xc = _causal_conv(x_ref[...], prev, cw_ref, cb_ref)
    xcb = xc.astype(BF16)
    r_parts, i_parts = [], []
    for g in range(N_GROUPS):
        gs = slice(g * GROUP_DIM, (g + 1) * GROUP_DIM)
        r_parts.append(_dot(xcb[:, gs], wr_ref[g]))
        i_parts.append(_dot(xcb[:, gs], wi_ref[g]))
    r = jax.nn.sigmoid(jnp.concatenate(r_parts, axis=1) + br_ref[...])
    ig = jax.nn.sigmoid(jnp.concatenate(i_parts, axis=1) + bi_ref[...])
    log_a = LRU_C * r * _log_sigmoid(lam_ref[...])
    a = jnp.exp(log_a)
    bx = jnp.sqrt(-jnp.tanh(log_a) * (a * a + 1.0)) * (ig * xc)
    a_cum, h_loc = _lin_scan(a, bx)
    h = h_loc + a_cum * h_s[0:1, :]
    t = h.shape[0]
    h_s[...] = jnp.broadcast_to(h[t - 1:t, :], h_s.shape)
    o_ref[...] = (h * jax.nn.gelu(gt_ref[...])).astype(o_ref.dtype)


def _rglru(proj, cw, cb, wr, br, wi, bi, lam, t):
    s = proj.shape[0]
    return pl.pallas_call(
        _rglru_kernel,
        grid=(s // t,),
        in_specs=[_col_spec(t, COL_B_X), _halo_spec(t, COL_B_X), _col_spec(t, COL_B_G),
                  _full_spec((CONV_W, BRANCH_W)), _full_spec((1, BRANCH_W)),
                  _full_spec((N_GROUPS, GROUP_DIM, GROUP_DIM)), _full_spec((1, BRANCH_W)),
                  _full_spec((N_GROUPS, GROUP_DIM, GROUP_DIM)), _full_spec((1, BRANCH_W)),
                  _full_spec((1, BRANCH_W))],
        out_specs=pl.BlockSpec((t, BRANCH_W), lambda i: (i, 0)),
        out_shape=jax.ShapeDtypeStruct((s, BRANCH_W), BF16),
        scratch_shapes=[pltpu.VMEM((8, BRANCH_W), F32)],
        compiler_params=_params("arbitrary"),
        name="rglru",
    )(proj, proj, proj, cw, cb, wr, br, wi, bi, lam)


def _mlstm_kernel(q_ref, qp_ref, k_ref, kp_ref, v_ref, og_ref, ifc_ref, ifr_ref,
                  cwq_ref, cbq_ref, cwk_ref, cbk_ref, bifc_ref, bifr_ref, gh_ref,
                  y_ref, ct_s, n_s, m_s):
    i = pl.program_id(0)

    @pl.when(i == 0)
    def _():
        ct_s[...] = jnp.zeros_like(ct_s)
        n_s[...] = jnp.zeros_like(n_s)
        m_s[...] = jnp.zeros_like(m_s)

    first = i == 0
    q = jax.nn.silu(_causal_conv(q_ref[...], jnp.where(first, 0.0, qp_ref[...]), cwq_ref, cbq_ref))
    k = jax.nn.silu(_causal_conv(k_ref[...], jnp.where(first, 0.0, kp_ref[...]), cwk_ref, cbk_ref))
    k = k * (GROUP_DIM ** -0.5)
    v = v_ref[...]
    og = og_ref[...]

    row = lax.broadcasted_iota(jnp.int32, (CHUNK, CHUNK), 0)
    col = lax.broadcasted_iota(jnp.int32, (CHUNK, CHUNK), 1)
    tril = col <= row
    tril_b = jnp.where(tril, 1.0, 0.0).astype(BF16)
    triu_b = jnp.where(row <= col, 1.0, 0.0).astype(BF16)
    state = [(ct_s[h], n_s[h:h + 1, :], m_s[h:h + 1, 0:1]) for h in range(N_GROUPS)]

    for c in range(q.shape[0] // CHUNK):
        cs = slice(c * CHUNK, (c + 1) * CHUNK)
        ifc = ifc_ref[cs, :] + bifc_ref[...]
        ifr = ifr_ref[:, cs] + bifr_ref[...]
        bcc = sum(_dot(tril_b, piece) for piece in _split3(_log_sigmoid(ifc)))
        bcr = sum(_dot(piece, triu_b) for piece in _split3(_log_sigmoid(ifr)))

        for h in range(N_GROUPS):
            hs = slice(h * GROUP_DIM, (h + 1) * GROUP_DIM)
            qh, kh, vh = q[cs, hs], k[cs, hs], v[cs, hs]
            ic_row, ic_col = ifr[h:h + 1, :], ifc[:, h:h + 1]
            bc_row, bc_col = bcr[N_GROUPS + h:N_GROUPS + h + 1, :], bcc[:, N_GROUPS + h:N_GROUPS + h + 1]
            ct, n_row, m_st = state[h]

            dmat = jnp.where(tril, bc_col - bc_row + ic_row, -jnp.inf)
            m_inter = bc_col + m_st
            m_t = jnp.maximum(m_inter, jnp.max(dmat, axis=-1, keepdims=True))
            kt = kh.T
            qb, vb = qh.astype(BF16), vh.astype(BF16)
            p = _dot(qb, kt.astype(BF16)) * jnp.exp(dmat - m_t)
            sc = jnp.exp(m_inter - m_t)
            num = _dot(p.astype(BF16), vb) + sc * _dot(qb, ct.astype(BF16))
            den = jnp.sum(p, axis=-1, keepdims=True) + sc * jnp.sum(qh * n_row, axis=-1, keepdims=True)
            hh = num / jnp.maximum(jnp.abs(den), jnp.exp(-m_t))

            b_last = bc_row[:, CHUNK - 1:CHUNK]
            g_row = b_last - bc_row + ic_row
            g_col = b_last - bc_col + ic_col
            m_new = jnp.maximum(b_last + m_st, jnp.max(g_row, axis=-1, keepdims=True))
            w_row = jnp.exp(g_row - m_new)
            w_col = jnp.exp(g_col - m_new)
            decay = jnp.exp(b_last + m_st - m_new)
            state[h] = (decay * ct + _dot((kt * w_row).astype(BF16), vb),
                        decay * n_row + jnp.sum(kh * w_col, axis=0, keepdims=True),
                        m_new)

            hn = _rms(hh) * gh_ref[:, hs]
            y_ref[cs, hs] = (jax.nn.sigmoid(og[cs, hs]) * hn).astype(y_ref.dtype)

    for h in range(N_GROUPS):
        ct_s[h] = state[h][0]
        n_s[h:h + 1, :] = state[h][1]
        m_s[h:h + 1, :] = jnp.broadcast_to(state[h][2], (1, GROUP_DIM))


def _mlstm(proj, ifc, ifr, cwq, cbq, cwk, cbk, bifc, bifr, gh, t):
    s = proj.shape[0]
    return pl.pallas_call(
        _mlstm_kernel,
        grid=(s // t,),
        in_specs=[_col_spec(t, COL_C_Q), _halo_spec(t, COL_C_Q),
                  _col_spec(t, COL_C_K), _halo_spec(t, COL_C_K),
                  _col_spec(t, COL_C_V), _col_spec(t, COL_C_O),
                  pl.BlockSpec((t, IF_PAD), lambda i: (i, 0)),
                  pl.BlockSpec((IF_ROWS, t), lambda i: (0, i)),
                  _full_spec((CONV_W, BRANCH_W)), _full_spec((1, BRANCH_W)),
                  _full_spec((CONV_W, BRANCH_W)), _full_spec((1, BRANCH_W)),
                  _full_spec((1, IF_PAD)), _full_spec((IF_ROWS, 1)),
                  _full_spec((1, BRANCH_W))],
        out_specs=pl.BlockSpec((t, BRANCH_W), lambda i: (i, 0)),
        out_shape=jax.ShapeDtypeStruct((s, BRANCH_W), BF16),
        scratch_shapes=[pltpu.VMEM((N_GROUPS, GROUP_DIM, GROUP_DIM), F32),
                        pltpu.VMEM((8, GROUP_DIM), F32),
                        pltpu.VMEM((8, GROUP_DIM), F32)],
        compiler_params=_params("arbitrary"),
        name="mlstm",
    )(proj, proj, proj, proj, proj, proj, ifc, ifr, cwq, cbq, cwk, cbk, bifc, bifr, gh)


def _sb_kernel(q_ref, k_ref, v_ref, o_ref, *, tq, sub):
    qi = pl.program_id(1)
    nsub = tq // sub
    q = q_ref[0]
    krow = lax.broadcasted_iota(jnp.int32, (sub, sub), 0)
    kcol = lax.broadcasted_iota(jnp.int32, (sub, sub), 1)
    from_j = jnp.where(krow >= kcol, 1.0, 0.0).astype(BF16)

    def piece(qa, start, strict, run):
        kb = k_ref[0, pl.ds(start, sub), :]
        z = _dot_nt(qa, kb)
        neg_abs = lax.bitcast_convert_type(
            lax.bitcast_convert_type(z, jnp.uint32) | jnp.uint32(0x80000000), F32)
        sp = jnp.maximum(z, 0.0) + jnp.log(1.0 + jnp.exp2(neg_abs)) * LOG2E
        if strict is not None:
            sp = jnp.where(strict, sp, 0.0)
        att = jnp.exp2(z - _dot(sp.astype(BF16), from_j) - run)
        if strict is not None:
            att = jnp.where(strict, att, 0.0)
        return att.astype(BF16), run + jnp.sum(sp, axis=-1, keepdims=True)

    acc = jnp.zeros((tq, GROUP_DIM), F32)
    run = jnp.zeros((tq, 1), F32)
    for b in range(nsub - 1, -1, -1):
        r0 = b * sub
        qpos = lax.broadcasted_iota(jnp.int32, (tq - r0, sub), 0)
        kpos = lax.broadcasted_iota(jnp.int32, (tq - r0, sub), 1)
        start = pl.multiple_of(qi * tq + r0, sub)
        att, run_b = piece(q[r0:], start, kpos < qpos, run[r0:])
        acc_b = acc[r0:] + _dot(att, v_ref[0, pl.ds(start, sub), :])
        acc = acc_b if b == 0 else jnp.concatenate([acc[:r0], acc_b], axis=0)
        run = run_b if b == 0 else jnp.concatenate([run[:r0], run_b], axis=0)

    def tile(j, carry):
        acc, run = carry
        base = (qi - 1 - j) * tq
        atts = []
        for b in range(nsub - 1, -1, -1):
            att, run = piece(q, pl.multiple_of(base + b * sub, sub), None, run)
            atts.append(att)
        vb = v_ref[0, pl.ds(pl.multiple_of(base, tq), tq), :]
        return acc + _dot(jnp.concatenate(atts[::-1], axis=1), vb), run

    acc, run = lax.fori_loop(0, qi, tile, (acc, run))
    o_ref[...] = acc.astype(o_ref.dtype)


def _sb_attention(qh, kh, vh, tq):
    _, s, _ = qh.shape
    return pl.pallas_call(
        functools.partial(_sb_kernel, tq=tq, sub=min(tq, 256)),
        grid=(N_GROUPS, s // tq),
        in_specs=[pl.BlockSpec((1, tq, GROUP_DIM), lambda h, i: (h, i, 0)),
                  pl.BlockSpec((1, s, GROUP_DIM), lambda h, i: (h, 0, 0)),
                  pl.BlockSpec((1, s, GROUP_DIM), lambda h, i: (h, 0, 0))],
        out_specs=pl.BlockSpec((tq, GROUP_DIM), lambda h, i: (i, h)),
        out_shape=jax.ShapeDtypeStruct((s, BRANCH_W), BF16),
        compiler_params=_params("parallel", "parallel"),
        name="sb_attn",
    )(qh, kh, vh)


def _tile(s, pref):
    return min(s, pref)


def _layer(x, p):
    s = x.shape[0]
    l, st = p["layer"], p["stacks"]
    xn, ifc, ifr = _norm_if(x, p["norm_mix"], p["w_if"], p["w_if_t"], _tile(s, 512))
    proj = _matmul(xn, st["w_in_abc_bf16"], l, F32, _tile(s, 1024), 512, "in_proj")
    qh, kh, vh = _qkv_proj(xn, st["w_in_d_bf16"], l, p["sb_norm_q"], p["sb_norm_k"], _tile(s, 1024))

    y_a = _gmlp(proj, p["gm_norm_v"], p["gm_w_s"], p["gm_b_s_t"], _tile(s, 512))
    y_b = _rglru(proj, p["lru_conv_w"], p["lru_conv_b"], p["lru_w_r"], p["lru_b_r"],
                 p["lru_w_i"], p["lru_b_i"], p["lru_lambda"], _tile(s, 512))
    y_c = _mlstm(proj, ifc, ifr, p["ml_conv_w_q"], p["ml_conv_b_q"], p["ml_conv_w_k"], p["ml_conv_b_k"],
                 p["ml_b_if_c"], p["ml_b_if_r"], p["ml_norm_h"], CHUNK)
    y_d = _sb_attention(qh, kh, vh, _tile(s, 1024))

    merged = _merge(xn, (y_a, y_b, y_c, y_d), st["w_gate"], st["b_gate"], st["w_branch"], l, _tile(s, 1024), 256)
    x, hn = _out_proj_norm(merged, st["w_out_bf16"], l, x, p["norm_ffn"], _tile(s, 512))
    hid = _ffn_up(hn, st["w_ffn_gate"], st["w_ffn_up"], l, _tile(s, 1024), 512)
    return _matmul_residual(hid, st["w_ffn_down_bf16"], l, x, _tile(s, 1024), 512, "ffn_down")


def _prepare_layer(l, stacks, norm_mix, w_in, gm_norm_v, gm_w_s, gm_b_s, lru_conv_w, lru_conv_b, lru_w_r, lru_b_r,
                   lru_w_i, lru_b_i, lru_lambda, ml_conv_w, ml_conv_b, ml_b_i, ml_b_f, ml_norm_h, sb_norm_q,
                   sb_norm_k, w_branch, w_gate, b_gate, w_out, norm_ffn, w_ffn_gate, w_ffn_up, w_ffn_down):
    w = BRANCH_W
    n_main = 8 * w
    w_if = w_in[l, :, n_main:n_main + 2 * N_GROUPS].astype(BF16)
    w_if_c = jnp.pad(w_if, ((0, 0), (0, IF_PAD - 2 * N_GROUPS)))
    w_if_r = jnp.pad(w_if.T, ((0, IF_ROWS - 2 * N_GROUPS), (0, 0)))
    b_if = jnp.concatenate([ml_b_i[l], ml_b_f[l]])
    return {
        "layer": l, "stacks": stacks,
        "norm_mix": norm_mix[l][None, :],
        "w_if": w_if_c, "w_if_t": w_if_r,
        "gm_norm_v": gm_norm_v[l][None, :],
        "gm_w_s": gm_w_s[l],
        "gm_b_s_t": gm_b_s[l].T,
        "lru_conv_w": lru_conv_w[l], "lru_conv_b": lru_conv_b[l][None, :],
        "lru_w_r": lru_w_r[l].astype(BF16), "lru_b_r": lru_b_r[l][None, :],
        "lru_w_i": lru_w_i[l].astype(BF16), "lru_b_i": lru_b_i[l][None, :],
        "lru_lambda": lru_lambda[l][None, :],
        "ml_conv_w_q": ml_conv_w[l][:, :w], "ml_conv_b_q": ml_conv_b[l][None, :w],
        "ml_conv_w_k": ml_conv_w[l][:, w:], "ml_conv_b_k": ml_conv_b[l][None, w:],
        "ml_b_if_c": jnp.pad(b_if, (0, IF_PAD - 2 * N_GROUPS))[None, :],
        "ml_b_if_r": jnp.pad(b_if, (0, IF_ROWS - 2 * N_GROUPS))[:, None],
        "ml_norm_h": ml_norm_h[l].reshape(1, w),
        "sb_norm_q": sb_norm_q[l][None, :], "sb_norm_k": sb_norm_k[l][None, :],
        "norm_ffn": norm_ffn[l][None, :],
    }


def kernel(x, norm_mix, w_in, gm_norm_v, gm_w_s, gm_b_s, lru_conv_w, lru_conv_b, lru_w_r, lru_b_r, lru_w_i, lru_b_i, lru_lambda, ml_conv_w, ml_conv_b, ml_b_i, ml_b_f, ml_norm_h, sb_norm_q, sb_norm_k, w_branch, w_gate, b_gate, w_out, norm_ffn, w_ffn_gate, w_ffn_up, w_ffn_down):
    b, s, d = x.shape
    n_abc = 8 * BRANCH_W
    stacks = {"w_in_abc_bf16": w_in[:, :, :n_abc].astype(BF16),
              "w_in_d_bf16": w_in[:, :, n_abc + 2 * N_GROUPS:].astype(BF16),
              "w_gate": w_gate, "b_gate": b_gate, "w_branch": w_branch, "w_out_bf16": w_out.astype(BF16),
              "w_ffn_gate": w_ffn_gate, "w_ffn_up": w_ffn_up, "w_ffn_down_bf16": w_ffn_down.astype(BF16)}
    outs = []
    for bi in range(b):
        xb = x.reshape(s, d) if b == 1 else x[bi]
        for l in range(norm_mix.shape[0]):
            p = _prepare_layer(l, stacks, norm_mix, w_in, gm_norm_v, gm_w_s, gm_b_s, lru_conv_w, lru_conv_b,
                               lru_w_r, lru_b_r, lru_w_i, lru_b_i, lru_lambda, ml_conv_w, ml_conv_b, ml_b_i,
                               ml_b_f, ml_norm_h, sb_norm_q, sb_norm_k, w_branch, w_gate, b_gate, w_out,
                               norm_ffn, w_ffn_gate, w_ffn_up, w_ffn_down)
            xb = _layer(xb, p)
        outs.append(xb)
    return outs[0].reshape(b, s, d) if b == 1 else jnp.stack(outs, axis=0)
```

```python
import functools

import jax
import jax.numpy as jnp
from jax import lax
from jax.experimental import pallas as pl
from jax.experimental.pallas import tpu as pltpu

D_MODEL = 2048
N_BRANCH = 4
BRANCH_W = D_MODEL // N_BRANCH
GROUP_DIM = 128
N_GROUPS = BRANCH_W // GROUP_DIM
CHUNK = 128
CONV_W = 4
LRU_C = 8.0
EPS = 1e-6
IF_PAD = 128
IF_ROWS = 16
LOG2E = 1.4426950408889634

F32 = jnp.float32
BF16 = jnp.bfloat16

VMEM_LIMIT_BYTES = 48 * 1024 * 1024

COL_A_U, COL_A_V, COL_B_X, COL_B_G, COL_C_Q, COL_C_K, COL_C_V, COL_C_O = range(8)


def _params(*sem):
    return pltpu.CompilerParams(dimension_semantics=sem, vmem_limit_bytes=VMEM_LIMIT_BYTES)


def _dot(a, b):
    return jnp.dot(a, b, preferred_element_type=F32)


def _dot_nt(a, b):
    return lax.dot_general(a, b, (((1,), (1,)), ((), ())), preferred_element_type=F32)


def _log_sigmoid(x):
    return jnp.minimum(x, 0.0) - jnp.log1p(jnp.exp(-jnp.abs(x)))


def _rms(x):
    return x * lax.rsqrt(jnp.mean(x * x, axis=-1, keepdims=True) + EPS)


def _split3(a):
    a1 = a.astype(BF16)
    r1 = a - a1.astype(F32)
    a2 = r1.astype(BF16)
    a3 = (r1 - a2.astype(F32)).astype(BF16)
    return a1, a2, a3


def _norm_if_kernel(x_ref, g_ref, wif_ref, wift_ref, xn_ref, ifc_ref, ifr_ref):
    xn = (_rms(x_ref[...]) * g_ref[...]).astype(BF16)
    xn_ref[...] = xn
    ifc_ref[...] = _dot(xn, wif_ref[...])
    ifr_ref[...] = _dot_nt(wift_ref[...], xn)


def _norm_if(x, g, wif, wift, tm):
    s, d = x.shape
    return pl.pallas_call(
        _norm_if_kernel,
        grid=(s // tm,),
        in_specs=[pl.BlockSpec((tm, d), lambda i: (i, 0)),
                  pl.BlockSpec((1, d), lambda i: (0, 0)),
                  pl.BlockSpec((d, IF_PAD), lambda i: (0, 0)),
                  pl.BlockSpec((IF_ROWS, d), lambda i: (0, 0))],
        out_specs=[pl.BlockSpec((tm, d), lambda i: (i, 0)),
                   pl.BlockSpec((tm, IF_PAD), lambda i: (i, 0)),
                   pl.BlockSpec((IF_ROWS, tm), lambda i: (0, i))],
        out_shape=[jax.ShapeDtypeStruct((s, d), BF16),
                   jax.ShapeDtypeStruct((s, IF_PAD), F32),
                   jax.ShapeDtypeStruct((IF_ROWS, s), F32)],
        compiler_params=_params("parallel"),
        name="norm_if",
    )(x, g, wif, wift)


def _mm_kernel(x_ref, w_ref, o_ref):
    o_ref[...] = _dot(x_ref[...], w_ref[...]).astype(o_ref.dtype)


def _mm_res_kernel(x_ref, w_ref, r_ref, o_ref):
    o_ref[...] = r_ref[...] + _dot(x_ref[...], w_ref[...])


def _matmul(x, w_stack, layer, out_dtype, tm, tn, name):
    m, k = x.shape
    n = w_stack.shape[2]
    return pl.pallas_call(
        _mm_kernel,
        grid=(m // tm, n // tn),
        in_specs=[pl.BlockSpec((tm, k), lambda i, j: (i, 0)),
                  pl.BlockSpec((None, k, tn), lambda i, j: (layer, 0, j))],
        out_specs=pl.BlockSpec((tm, tn), lambda i, j: (i, j)),
        out_shape=jax.ShapeDtypeStruct((m, n), out_dtype),
        compiler_params=_params("parallel", "parallel"),
        name=name,
    )(x, w_stack)


def _matmul_residual(x, w_stack, layer, res, tm, tn, name):
    m, k = x.shape
    n = w_stack.shape[2]
    return pl.pallas_call(
        _mm_res_kernel,
        grid=(m // tm, n // tn),
        in_specs=[pl.BlockSpec((tm, k), lambda i, j: (i, 0)),
                  pl.BlockSpec((None, k, tn), lambda i, j: (layer, 0, j)),
                  pl.BlockSpec((tm, tn), lambda i, j: (i, j))],
        out_specs=pl.BlockSpec((tm, tn), lambda i, j: (i, j)),
        out_shape=jax.ShapeDtypeStruct((m, n), F32),
        compiler_params=_params("parallel", "parallel"),
        name=name,
    )(x, w_stack, res)


def _out_norm_kernel(x_ref, w_ref, r_ref, g_ref, o_ref, hn_ref):
    xo = r_ref[...] + _dot(x_ref[...], w_ref[...])
    o_ref[...] = xo
    hn_ref[...] = (_rms(xo) * g_ref[...]).astype(BF16)


def _out_proj_norm(merged, w_stack, layer, res, g, tm):
    s, d = merged.shape
    row_spec = pl.BlockSpec((tm, d), lambda i: (i, 0))
    return pl.pallas_call(
        _out_norm_kernel,
        grid=(s // tm,),
        in_specs=[row_spec, pl.BlockSpec((None, d, d), lambda i: (layer, 0, 0)), row_spec,
                  pl.BlockSpec((1, d), lambda i: (0, 0))],
        out_specs=[row_spec, row_spec],
        out_shape=[jax.ShapeDtypeStruct((s, d), F32), jax.ShapeDtypeStruct((s, d), BF16)],
        compiler_params=_params("parallel"),
        name="out_proj_norm",
    )(merged, w_stack, res, g)


def _qkv_kernel(x_ref, w_ref, gq_ref, gk_ref, qo_ref, ko_ref, vo_ref):
    j = pl.program_id(1)
    acc = _dot(x_ref[...], w_ref[...])

    def heads(fn, out_ref):
        for h in range(N_GROUPS):
            out_ref[h] = fn(acc[:, h * GROUP_DIM:(h + 1) * GROUP_DIM]).astype(BF16)

    @pl.when(j == 0)
    def _():
        heads(lambda t: _rms(t) * gq_ref[...] * (GROUP_DIM ** -0.5 * LOG2E), qo_ref)

    @pl.when(j == 1)
    def _():
        heads(lambda t: _rms(t) * gk_ref[...], ko_ref)

    @pl.when(j == 2)
    def _():
        heads(lambda t: t, vo_ref)


def _qkv_proj(xn, w_stack, layer, gq, gk, tm):
    s, d = xn.shape
    hm_spec = pl.BlockSpec((N_GROUPS, tm, GROUP_DIM), lambda i, j: (0, i, 0))
    hm_shape = jax.ShapeDtypeStruct((N_GROUPS, s, GROUP_DIM), BF16)
    g_spec = pl.BlockSpec((1, GROUP_DIM), lambda i, j: (0, 0))
    return pl.pallas_call(
        _qkv_kernel,
        grid=(s // tm, 3),
        in_specs=[pl.BlockSpec((tm, d), lambda i, j: (i, 0)),
                  pl.BlockSpec((None, d, BRANCH_W), lambda i, j: (layer, 0, j)), g_spec, g_spec],
        out_specs=[hm_spec, hm_spec, hm_spec],
        out_shape=[hm_shape, hm_shape, hm_shape],
        compiler_params=_params("arbitrary", "arbitrary"),
        name="qkv_proj",
    )(xn, w_stack, gq, gk)


def _cast_at_first_row_tile(pairs):
    @pl.when(pl.program_id(1) == 0)
    def _():
        for w_ref, wb_s in pairs:
            wb_s[...] = w_ref[...].astype(BF16)


def _merge_kernel(xn_ref, ya_ref, yb_ref, yc_ref, yd_ref, wg_ref, bg_ref, wb_ref, o_ref, wg_s, wb_s):
    _cast_at_first_row_tile([(wg_ref, wg_s), (wb_ref, wb_s)])
    xn = xn_ref[...]
    acc = None
    for g, y_ref in enumerate((ya_ref, yb_ref, yc_ref, yd_ref)):
        gate = jax.nn.sigmoid(_dot(xn, wg_s[g]) + bg_ref[g:g + 1, :])
        term = gate * _dot(y_ref[...], wb_s[g])
        acc = term if acc is None else acc + term
    o_ref[...] = acc.astype(o_ref.dtype)


def _merge(xn, ys, wg_stack, bg_stack, wb_stack, layer, tm, tn):
    s, d = xn.shape
    w = ys[0].shape[1]
    y_spec = pl.BlockSpec((tm, w), lambda j, i: (i, 0))
    return pl.pallas_call(
        _merge_kernel,
        grid=(d // tn, s // tm),
        in_specs=[pl.BlockSpec((tm, d), lambda j, i: (i, 0)),
                  y_spec, y_spec, y_spec, y_spec,
                  pl.BlockSpec((None, N_BRANCH, d, tn), lambda j, i: (layer, 0, 0, j)),
                  pl.BlockSpec((None, N_BRANCH, tn), lambda j, i: (layer, 0, j)),
                  pl.BlockSpec((None, N_BRANCH, w, tn), lambda j, i: (layer, 0, 0, j))],
        out_specs=pl.BlockSpec((tm, tn), lambda j, i: (i, j)),
        out_shape=jax.ShapeDtypeStruct((s, d), BF16),
        scratch_shapes=[pltpu.VMEM((N_BRANCH, d, tn), BF16), pltpu.VMEM((N_BRANCH, w, tn), BF16)],
        compiler_params=_params("arbitrary", "arbitrary"),
        name="merge",
    )(xn, *ys, wg_stack, bg_stack, wb_stack)


def _ffn_up_kernel(x_ref, wg_ref, wu_ref, o_ref, wg_s, wu_s):
    _cast_at_first_row_tile([(wg_ref, wg_s), (wu_ref, wu_s)])
    x = x_ref[...]
    o_ref[...] = (jax.nn.silu(_dot(x, wg_s[...])) * _dot(x, wu_s[...])).astype(o_ref.dtype)


def _ffn_up(hn, wg_stack, wu_stack, layer, tm, tn):
    s, d = hn.shape
    f = wg_stack.shape[2]
    w_spec = pl.BlockSpec((None, d, tn), lambda j, i: (layer, 0, j))
    return pl.pallas_call(
        _ffn_up_kernel,
        grid=(f // tn, s // tm),
        in_specs=[pl.BlockSpec((tm, d), lambda j, i: (i, 0)), w_spec, w_spec],
        out_specs=pl.BlockSpec((tm, tn), lambda j, i: (i, j)),
        out_shape=jax.ShapeDtypeStruct((s, f), BF16),
        scratch_shapes=[pltpu.VMEM((d, tn), BF16), pltpu.VMEM((d, tn), BF16)],
        compiler_params=_params("arbitrary", "arbitrary"),
        name="ffn_up",
    )(hn, wg_stack, wu_stack)


def _col_spec(t, col):
    return pl.BlockSpec((t, BRANCH_W), lambda i: (i, col))


def _halo_spec(t, col):
    return pl.BlockSpec((8, BRANCH_W), lambda i: (jnp.maximum(i * (t // 8) - 1, 0), col))


def _full_spec(shape):
    nd = len(shape)
    return pl.BlockSpec(shape, lambda i: (0,) * nd)


def _shift_rows(x, prev8, j):
    xr = pltpu.roll(x, j, 0)
    pr = pltpu.roll(prev8, j, 0)
    rid = lax.broadcasted_iota(jnp.int32, pr.shape, 0)
    top = jnp.where(rid < j, pr, xr[:8])
    return jnp.concatenate([top, xr[8:]], axis=0)


def _causal_conv(x, prev8, w_ref, b_ref):
    acc = None
    for k in range(CONV_W):
        j = CONV_W - 1 - k
        xs = x if j == 0 else _shift_rows(x, prev8, j)
        term = xs * w_ref[k:k + 1, :]
        acc = term if acc is None else acc + term
    return acc + b_ref[...]


def _gmlp_kernel(u_ref, v_ref, gv_ref, ws_ref, bst_ref, o_ref):
    t = u_ref.shape[0]
    u = jax.nn.gelu(u_ref[...])
    v = (_rms(jax.nn.gelu(v_ref[...])) * gv_ref[...]).astype(BF16)
    row = lax.broadcasted_iota(jnp.int32, (CHUNK, CHUNK), 0)
    col = lax.broadcasted_iota(jnp.int32, (CHUNK, CHUNK), 1)
    for g in range(N_GROUPS):
        gs = slice(g * GROUP_DIM, (g + 1) * GROUP_DIM)
        ws = jnp.where(col <= row, ws_ref[g], 0.0).astype(BF16)
        bcol = bst_ref[:, g:g + 1]
        for c in range(t // CHUNK):
            cs = slice(c * CHUNK, (c + 1) * CHUNK)
            sp = _dot(ws, v[cs, gs]) + bcol
            o_ref[cs, gs] = (u[cs, gs] * sp).astype(o_ref.dtype)


def _gmlp(proj, gv, ws, bst, t):
    s = proj.shape[0]
    return pl.pallas_call(
        _gmlp_kernel,
        grid=(s // t,),
        in_specs=[_col_spec(t, COL_A_U), _col_spec(t, COL_A_V),
                  _full_spec((1, BRANCH_W)), _full_spec((N_GROUPS, CHUNK, CHUNK)),
                  _full_spec((CHUNK, N_GROUPS))],
        out_specs=pl.BlockSpec((t, BRANCH_W), lambda i: (i, 0)),
        out_shape=jax.ShapeDtypeStruct((s, BRANCH_W), BF16),
        compiler_params=_params("parallel"),
        name="gmlp",
    )(proj, proj, gv, ws, bst)


def _lin_scan(a, b):
    t = a.shape[0]
    rid = lax.broadcasted_iota(jnp.int32, a.shape, 0)
    d = 1
    while d < t:
        keep = rid >= d
        a_sh = jnp.where(keep, pltpu.roll(a, d, 0), 1.0)
        b_sh = jnp.where(keep, pltpu.roll(b, d, 0), 0.0)
        b = a * b_sh + b
        a = a * a_sh
        d *= 2
    return a, b


def _rglru_kernel(x_ref, xp_ref, gt_ref, cw_ref, cb_ref, wr_ref, br_ref, wi_ref, bi_ref, lam_ref,
                  o_ref, h_s):
    i = pl.program_id(0)

    @pl.when(i == 0)
    def _():
        h_s[...] = jnp.zeros_like(h_s)

    prev = jnp.where(i == 0, 0.0, xp_ref[...])
    xc = _causal_conv(x_ref[...], prev, cw_ref, cb_ref)
    xcb = xc.astype(BF16)
    r_parts, i_parts = [], []
    for g in range(N_GROUPS):
        gs = slice(g * GROUP_DIM, (g + 1) * GROUP_DIM)
        r_parts.append(_dot(xcb[:, gs], wr_ref[g]))
        i_parts.append(_dot(xcb[:, gs], wi_ref[g]))
    r = jax.nn.sigmoid(jnp.concatenate(r_parts, axis=1) + br_ref[...])
    ig = jax.nn.sigmoid(jnp.concatenate(i_parts, axis=1) + bi_ref[...])
    log_a = LRU_C * r * _log_sigmoid(lam_ref[...])
    a = jnp.exp(log_a)
    bx = jnp.sqrt(-jnp.tanh(log_a) * (a * a + 1.0)) * (ig * xc)
    a_cum, h_loc = _lin_scan(a, bx)
    h = h_loc + a_cum * h_s[0:1, :]
    t = h.shape[0]
    h_s[...] = jnp.broadcast_to(h[t - 1:t, :], h_s.shape)
    o_ref[...] = (h * jax.nn.gelu(gt_ref[...])).astype(o_ref.dtype)


def _rglru(proj, cw, cb, wr, br, wi, bi, lam, t):
    s = proj.shape[0]
    return pl.pallas_call(
        _rglru_kernel,
        grid=(s // t,),
        in_specs=[_col_spec(t, COL_B_X), _halo_spec(t, COL_B_X), _col_spec(t, COL_B_G),
                  _full_spec((CONV_W, BRANCH_W)), _full_spec((1, BRANCH_W)),
                  _full_spec((N_GROUPS, GROUP_DIM, GROUP_DIM)), _full_spec((1, BRANCH_W)),
                  _full_spec((N_GROUPS, GROUP_DIM, GROUP_DIM)), _full_spec((1, BRANCH_W)),
                  _full_spec((1, BRANCH_W))],
        out_specs=pl.BlockSpec((t, BRANCH_W), lambda i: (i, 0)),
        out_shape=jax.ShapeDtypeStruct((s, BRANCH_W), BF16),
        scratch_shapes=[pltpu.VMEM((8, BRANCH_W), F32)],
        compiler_params=_params("arbitrary"),
        name="rglru",
    )(proj, proj, proj, cw, cb, wr, br, wi, bi, lam)


def _mlstm_kernel(q_ref, qp_ref, k_ref, kp_ref, v_ref, og_ref, ifc_ref, ifr_ref,
                  cwq_ref, cbq_ref, cwk_ref, cbk_ref, bifc_ref, bifr_ref, gh_ref,
                  y_ref, ct_s, n_s, m_s):
    i = pl.program_id(0)

    @pl.when(i == 0)
    def _():
        ct_s[...] = jnp.zeros_like(ct_s)
        n_s[...] = jnp.zeros_like(n_s)
        m_s[...] = jnp.zeros_like(m_s)

    first = i == 0
    q = jax.nn.silu(_causal_conv(q_ref[...], jnp.where(first, 0.0, qp_ref[...]), cwq_ref, cbq_ref))
    k = jax.nn.silu(_causal_conv(k_ref[...], jnp.where(first, 0.0, kp_ref[...]), cwk_ref, cbk_ref))
    k = k * (GROUP_DIM ** -0.5)
    v = v_ref[...]
    og = og_ref[...]

    row = lax.broadcasted_iota(jnp.int32, (CHUNK, CHUNK), 0)
    col = lax.broadcasted_iota(jnp.int32, (CHUNK, CHUNK), 1)
    tril = col <= row
    tril_b = jnp.where(tril, 1.0, 0.0).astype(BF16)
    triu_b = jnp.where(row <= col, 1.0, 0.0).astype(BF16)
    state = [(ct_s[h], n_s[h:h + 1, :], m_s[h:h + 1, 0:1]) for h in range(N_GROUPS)]

    for c in range(q.shape[0] // CHUNK):
        cs = slice(c * CHUNK, (c + 1) * CHUNK)
        ifc = ifc_ref[cs, :] + bifc_ref[...]
        ifr = ifr_ref[:, cs] + bifr_ref[...]
        bcc = sum(_dot(tril_b, piece) for piece in _split3(_log_sigmoid(ifc)))
        bcr = sum(_dot(piece, triu_b) for piece in _split3(_log_sigmoid(ifr)))

        for h in range(N_GROUPS):
            hs = slice(h * GROUP_DIM, (h + 1) * GROUP_DIM)
            qh, kh, vh = q[cs, hs], k[cs, hs], v[cs, hs]
            ic_row, ic_col = ifr[h:h + 1, :], ifc[:, h:h + 1]
            bc_row, bc_col = bcr[N_GROUPS + h:N_GROUPS + h + 1, :], bcc[:, N_GROUPS + h:N_GROUPS + h + 1]
            ct, n_row, m_st = state[h]

            dmat = jnp.where(tril, bc_col - bc_row + ic_row, -jnp.inf)
            m_inter = bc_col + m_st
            m_t = jnp.maximum(m_inter, jnp.max(dmat, axis=-1, keepdims=True))
            kt = kh.T
            qb, vb = qh.astype(BF16), vh.astype(BF16)
            p = _dot(qb, kt.astype(BF16)) * jnp.exp(dmat - m_t)
            sc = jnp.exp(m_inter - m_t)
            num = _dot(p.astype(BF16), vb) + sc * _dot(qb, ct.astype(BF16))
            den = jnp.sum(p, axis=-1, keepdims=True) + sc * jnp.sum(qh * n_row, axis=-1, keepdims=True)
            hh = num / jnp.maximum(jnp.abs(den), jnp.exp(-m_t))

            b_last = bc_row[:, CHUNK - 1:CHUNK]
            g_row = b_last - bc_row + ic_row
            g_col = b_last - bc_col + ic_col
            m_new = jnp.maximum(b_last + m_st, jnp.max(g_row, axis=-1, keepdims=True))
            w_row = jnp.exp(g_row - m_new)
            w_col = jnp.exp(g_col - m_new)
            decay = jnp.exp(b_last + m_st - m_new)
            state[h] = (decay * ct + _dot((kt * w_row).astype(BF16), vb),
                        decay * n_row + jnp.sum(kh * w_col, axis=0, keepdims=True),
                        m_new)

            hn = _rms(hh) * gh_ref[:, hs]
            y_ref[cs, hs] = (jax.nn.sigmoid(og[cs, hs]) * hn).astype(y_ref.dtype)

    for h in range(N_GROUPS):
        ct_s[h] = state[h][0]
        n_s[h:h + 1, :] = state[h][1]
        m_s[h:h + 1, :] = jnp.broadcast_to(state[h][2], (1, GROUP_DIM))


def _mlstm(proj, ifc, ifr, cwq, cbq, cwk, cbk, bifc, bifr, gh, t):
    s = proj.shape[0]
    return pl.pallas_call(
        _mlstm_kernel,
        grid=(s // t,),
        in_specs=[_col_spec(t, COL_C_Q), _halo_spec(t, COL_C_Q),
                  _col_spec(t, COL_C_K), _halo_spec(t, COL_C_K),
                  _col_spec(t, COL_C_V), _col_spec(t, COL_C_O),
                  pl.BlockSpec((t, IF_PAD), lambda i: (i, 0)),
                  pl.BlockSpec((IF_ROWS, t), lambda i: (0, i)),
                  _full_spec((CONV_W, BRANCH_W)), _full_spec((1, BRANCH_W)),
                  _full_spec((CONV_W, BRANCH_W)), _full_spec((1, BRANCH_W)),
                  _full_spec((1, IF_PAD)), _full_spec((IF_ROWS, 1)),
                  _full_spec((1, BRANCH_W))],
        out_specs=pl.BlockSpec((t, BRANCH_W), lambda i: (i, 0)),
        out_shape=jax.ShapeDtypeStruct((s, BRANCH_W), BF16),
        scratch_shapes=[pltpu.VMEM((N_GROUPS, GROUP_DIM, GROUP_DIM), F32),
                        pltpu.VMEM((8, GROUP_DIM), F32),
                        pltpu.VMEM((8, GROUP_DIM), F32)],
        compiler_params=_params("arbitrary"),
        name="mlstm",
    )(proj, proj, proj, proj, proj, proj, ifc, ifr, cwq, cbq, cwk, cbk, bifc, bifr, gh)


def _sb_kernel(q_ref, k_ref, v_ref, o_ref, *, tq, sub):
    qi = pl.program_id(1)
    nsub = tq // sub
    q = q_ref[0]
    krow = lax.broadcasted_iota(jnp.int32, (sub, sub), 0)
    kcol = lax.broadcasted_iota(jnp.int32, (sub, sub), 1)
    from_j = jnp.where(krow >= kcol, 1.0, 0.0).astype(BF16)

    def piece(qa, start, strict, run):
        kb = k_ref[0, pl.ds(start, sub), :]
        z = _dot_nt(qa, kb)
        neg_abs = lax.bitcast_convert_type(
            lax.bitcast_convert_type(z, jnp.uint32) | jnp.uint32(0x80000000), F32)
        sp = jnp.maximum(z, 0.0) + jnp.log(1.0 + jnp.exp2(neg_abs)) * LOG2E
        if strict is not None:
            sp = jnp.where(strict, sp, 0.0)
        att = jnp.exp2(z - _dot(sp.astype(BF16), from_j) - jnp.concatenate([run] * (sub // GROUP_DIM), axis=1))
        if strict is not None:
            att = jnp.where(strict, att, 0.0)
        return att.astype(BF16), run + jnp.sum(sp, axis=-1, keepdims=True)

    acc = jnp.zeros((tq, GROUP_DIM), F32)
    run = jnp.zeros((tq, GROUP_DIM), F32)
    for b in range(nsub - 1, -1, -1):
        r0 = b * sub
        qpos = lax.broadcasted_iota(jnp.int32, (tq - r0, sub), 0)
        kpos = lax.broadcasted_iota(jnp.int32, (tq - r0, sub), 1)
        start = pl.multiple_of(qi * tq + r0, sub)
        att, run_b = piece(q[r0:], start, kpos < qpos, run[r0:])
        acc_b = acc[r0:] + _dot(att, v_ref[0, pl.ds(start, sub), :])
        acc = acc_b if b == 0 else jnp.concatenate([acc[:r0], acc_b], axis=0)
        run = run_b if b == 0 else jnp.concatenate([run[:r0], run_b], axis=0)

    def tile(j, carry):
        acc, run = carry
        base = (qi - 1 - j) * tq
        atts = []
        for b in range(nsub - 1, -1, -1):
            att, run = piece(q, pl.multiple_of(base + b * sub, sub), None, run)
            atts.append(att)
        vb = v_ref[0, pl.ds(pl.multiple_of(base, tq), tq), :]
        return acc + _dot(jnp.concatenate(atts[::-1], axis=1), vb), run

    acc, run = lax.fori_loop(0, qi, tile, (acc, run))
    o_ref[...] = acc.astype(o_ref.dtype)


def _sb_attention(qh, kh, vh, tq):
    _, s, _ = qh.shape
    return pl.pallas_call(
        functools.partial(_sb_kernel, tq=tq, sub=min(tq, 256)),
        grid=(N_GROUPS, s // tq),
        in_specs=[pl.BlockSpec((1, tq, GROUP_DIM), lambda h, i: (h, i, 0)),
                  pl.BlockSpec((1, s, GROUP_DIM), lambda h, i: (h, 0, 0)),
                  pl.BlockSpec((1, s, GROUP_DIM), lambda h, i: (h, 0, 0))],
        out_specs=pl.BlockSpec((tq, GROUP_DIM), lambda h, i: (i, h)),
        out_shape=jax.ShapeDtypeStruct((s, BRANCH_W), BF16),
        compiler_params=_params("parallel", "parallel"),
        name="sb_attn",
    )(qh, kh, vh)


def _tile(s, pref):
    return min(s, pref)


def _layer(x, p):
    s = x.shape[0]
    l, st = p["layer"], p["stacks"]
    xn, ifc, ifr = _norm_if(x, p["norm_mix"], p["w_if"], p["w_if_t"], _tile(s, 512))
    proj = _matmul(xn, st["w_in_abc_bf16"], l, F32, _tile(s, 1024), 1024, "in_proj")
    qh, kh, vh = _qkv_proj(xn, st["w_in_d_bf16"], l, p["sb_norm_q"], p["sb_norm_k"], _tile(s, 1024))

    y_a = _gmlp(proj, p["gm_norm_v"], p["gm_w_s"], p["gm_b_s_t"], _tile(s, 512))
    y_b = _rglru(proj, p["lru_conv_w"], p["lru_conv_b"], p["lru_w_r"], p["lru_b_r"],
                 p["lru_w_i"], p["lru_b_i"], p["lru_lambda"], _tile(s, 512))
    y_c = _mlstm(proj, ifc, ifr, p["ml_conv_w_q"], p["ml_conv_b_q"], p["ml_conv_w_k"], p["ml_conv_b_k"],
                 p["ml_b_if_c"], p["ml_b_if_r"], p["ml_norm_h"], CHUNK)
    y_d = _sb_attention(qh, kh, vh, _tile(s, 1024))

    merged = _merge(xn, (y_a, y_b, y_c, y_d), st["w_gate"], st["b_gate"], st["w_branch"], l, _tile(s, 1024), 256)
    x, hn = _out_proj_norm(merged, st["w_out_bf16"], l, x, p["norm_ffn"], _tile(s, 512))
    hid = _ffn_up(hn, st["w_ffn_gate"], st["w_ffn_up"], l, _tile(s, 1024), 512)
    return _matmul_residual(hid, st["w_ffn_down_bf16"], l, x, _tile(s, 1024), 512, "ffn_down")


def _prepare_layer(l, stacks, norm_mix, w_in, gm_norm_v, gm_w_s, gm_b_s, lru_conv_w, lru_conv_b, lru_w_r, lru_b_r,
                   lru_w_i, lru_b_i, lru_lambda, ml_conv_w, ml_conv_b, ml_b_i, ml_b_f, ml_norm_h, sb_norm_q,
                   sb_norm_k, w_branch, w_gate, b_gate, w_out, norm_ffn, w_ffn_gate, w_ffn_up, w_ffn_down):
    w = BRANCH_W
    n_main = 8 * w
    w_if = w_in[l, :, n_main:n_main + 2 * N_GROUPS].astype(BF16)
    w_if_c = jnp.pad(w_if, ((0, 0), (0, IF_PAD - 2 * N_GROUPS)))
    w_if_r = jnp.pad(w_if.T, ((0, IF_ROWS - 2 * N_GROUPS), (0, 0)))
    b_if = jnp.concatenate([ml_b_i[l], ml_b_f[l]])
    return {
        "layer": l, "stacks": stacks,
        "norm_mix": norm_mix[l][None, :],
        "w_if": w_if_c, "w_if_t": w_if_r,
        "gm_norm_v": gm_norm_v[l][None, :],
        "gm_w_s": gm_w_s[l],
        "gm_b_s_t": gm_b_s[l].T,
        "lru_conv_w": lru_conv_w[l], "lru_conv_b": lru_conv_b[l][None, :],
        "lru_w_r": lru_w_r[l].astype(BF16), "lru_b_r": lru_b_r[l][None, :],
        "lru_w_i": lru_w_i[l].astype(BF16), "lru_b_i": lru_b_i[l][None, :],
        "lru_lambda": lru_lambda[l][None, :],
        "ml_conv_w_q": ml_conv_w[l][:, :w], "ml_conv_b_q": ml_conv_b[l][None, :w],
        "ml_conv_w_k": ml_conv_w[l][:, w:], "ml_conv_b_k": ml_conv_b[l][None, w:],
        "ml_b_if_c": jnp.pad(b_if, (0, IF_PAD - 2 * N_GROUPS))[None, :],
        "ml_b_if_r": jnp.pad(b_if, (0, IF_ROWS - 2 * N_GROUPS))[:, None],
        "ml_norm_h": ml_norm_h[l].reshape(1, w),
        "sb_norm_q": sb_norm_q[l][None, :], "sb_norm_k": sb_norm_k[l][None, :],
        "norm_ffn": norm_ffn[l][None, :],
    }


def kernel(x, norm_mix, w_in, gm_norm_v, gm_w_s, gm_b_s, lru_conv_w, lru_conv_b, lru_w_r, lru_b_r, lru_w_i, lru_b_i, lru_lambda, ml_conv_w, ml_conv_b, ml_b_i, ml_b_f, ml_norm_h, sb_norm_q, sb_norm_k, w_branch, w_gate, b_gate, w_out, norm_ffn, w_ffn_gate, w_ffn_up, w_ffn_down):
    b, s, d = x.shape
    n_abc = 8 * BRANCH_W
    stacks = {"w_in_abc_bf16": w_in[:, :, :n_abc].astype(BF16),
              "w_in_d_bf16": w_in[:, :, n_abc + 2 * N_GROUPS:].astype(BF16),
              "w_gate": w_gate, "b_gate": b_gate, "w_branch": w_branch, "w_out_bf16": w_out.astype(BF16),
              "w_ffn_gate": w_ffn_gate, "w_ffn_up": w_ffn_up, "w_ffn_down_bf16": w_ffn_down.astype(BF16)}
    outs = []
    for bi in range(b):
        xb = x.reshape(s, d) if b == 1 else x[bi]
        for l in range(norm_mix.shape[0]):
            p = _prepare_layer(l, stacks, norm_mix, w_in, gm_norm_v, gm_w_s, gm_b_s, lru_conv_w, lru_conv_b,
                               lru_w_r, lru_b_r, lru_w_i, lru_b_i, lru_lambda, ml_conv_w, ml_conv_b, ml_b_i,
                               ml_b_f, ml_norm_h, sb_norm_q, sb_norm_k, w_branch, w_gate, b_gate, w_out,
                               norm_ffn, w_ffn_gate, w_ffn_up, w_ffn_down)
            xb = _layer(xb, p)
        outs.append(xb)
    return outs[0].reshape(b, s, d) if b == 1 else jnp.stack(outs, axis=0)
```

```python
import functools

import jax
import jax.numpy as jnp
from jax import lax
from jax.experimental import pallas as pl
from jax.experimental.pallas import tpu as pltpu

D_MODEL = 2048
N_BRANCH = 4
BRANCH_W = D_MODEL // N_BRANCH
GROUP_DIM = 128
N_GROUPS = BRANCH_W // GROUP_DIM
CHUNK = 128
CONV_W = 4
LRU_C = 8.0
EPS = 1e-6
IF_PAD = 128
IF_ROWS = 16
LOG2E = 1.4426950408889634

F32 = jnp.float32
BF16 = jnp.bfloat16

VMEM_LIMIT_BYTES = 48 * 1024 * 1024

COL_A_U, COL_A_V, COL_B_X, COL_B_G, COL_C_Q, COL_C_K, COL_C_V, COL_C_O = range(8)


def _params(*sem):
    return pltpu.CompilerParams(dimension_semantics=sem, vmem_limit_bytes=VMEM_LIMIT_BYTES)


def _dot(a, b):
    return jnp.dot(a, b, preferred_element_type=F32)


def _dot_nt(a, b):
    return lax.dot_general(a, b, (((1,), (1,)), ((), ())), preferred_element_type=F32)


def _log_sigmoid(x):
    return jnp.minimum(x, 0.0) - jnp.log1p(jnp.exp(-jnp.abs(x)))


def _rms(x):
    return x * lax.rsqrt(jnp.mean(x * x, axis=-1, keepdims=True) + EPS)


def _split3(a):
    a1 = a.astype(BF16)
    r1 = a - a1.astype(F32)
    a2 = r1.astype(BF16)
    a3 = (r1 - a2.astype(F32)).astype(BF16)
    return a1, a2, a3


def _norm_if_kernel(x_ref, g_ref, wif_ref, wift_ref, xn_ref, ifc_ref, ifr_ref):
    xn = (_rms(x_ref[...]) * g_ref[...]).astype(BF16)
    xn_ref[...] = xn
    ifc_ref[...] = _dot(xn, wif_ref[...])
    ifr_ref[...] = _dot_nt(wift_ref[...], xn)


def _norm_if(x, g, wif, wift, tm):
    s, d = x.shape
    return pl.pallas_call(
        _norm_if_kernel,
        grid=(s // tm,),
        in_specs=[pl.BlockSpec((tm, d), lambda i: (i, 0)),
                  pl.BlockSpec((1, d), lambda i: (0, 0)),
                  pl.BlockSpec((d, IF_PAD), lambda i: (0, 0)),
                  pl.BlockSpec((IF_ROWS, d), lambda i: (0, 0))],
        out_specs=[pl.BlockSpec((tm, d), lambda i: (i, 0)),
                   pl.BlockSpec((tm, IF_PAD), lambda i: (i, 0)),
                   pl.BlockSpec((IF_ROWS, tm), lambda i: (0, i))],
        out_shape=[jax.ShapeDtypeStruct((s, d), BF16),
                   jax.ShapeDtypeStruct((s, IF_PAD), F32),
                   jax.ShapeDtypeStruct((IF_ROWS, s), F32)],
        compiler_params=_params("parallel"),
        name="norm_if",
    )(x, g, wif, wift)


def _mm_kernel(x_ref, w_ref, o_ref):
    o_ref[...] = _dot(x_ref[...], w_ref[...]).astype(o_ref.dtype)


def _mm_res_kernel(x_ref, w_ref, r_ref, o_ref):
    o_ref[...] = r_ref[...] + _dot(x_ref[...], w_ref[...])


def _matmul(x, w_stack, layer, out_dtype, tm, tn, name):
    m, k = x.shape
    n = w_stack.shape[2]
    return pl.pallas_call(
        _mm_kernel,
        grid=(m // tm, n // tn),
        in_specs=[pl.BlockSpec((tm, k), lambda i, j: (i, 0)),
                  pl.BlockSpec((None, k, tn), lambda i, j: (layer, 0, j))],
        out_specs=pl.BlockSpec((tm, tn), lambda i, j: (i, j)),
        out_shape=jax.ShapeDtypeStruct((m, n), out_dtype),
        compiler_params=_params("parallel", "parallel"),
        name=name,
    )(x, w_stack)


def _matmul_residual(x, w_stack, layer, res, tm, tn, name):
    m, k = x.shape
    n = w_stack.shape[2]
    return pl.pallas_call(
        _mm_res_kernel,
        grid=(m // tm, n // tn),
        in_specs=[pl.BlockSpec((tm, k), lambda i, j: (i, 0)),
                  pl.BlockSpec((None, k, tn), lambda i, j: (layer, 0, j)),
                  pl.BlockSpec((tm, tn), lambda i, j: (i, j))],
        out_specs=pl.BlockSpec((tm, tn), lambda i, j: (i, j)),
        out_shape=jax.ShapeDtypeStruct((m, n), F32),
        compiler_params=_params("parallel", "parallel"),
        name=name,
    )(x, w_stack, res)


def _out_norm_kernel(x_ref, w_ref, r_ref, g_ref, o_ref, hn_ref):
    xo = r_ref[...] + _dot(x_ref[...], w_ref[...])
    o_ref[...] = xo
    hn_ref[...] = (_rms(xo) * g_ref[...]).astype(BF16)


def _out_proj_norm(merged, w_stack, layer, res, g, tm):
    s, d = merged.shape
    row_spec = pl.BlockSpec((tm, d), lambda i: (i, 0))
    return pl.pallas_call(
        _out_norm_kernel,
        grid=(s // tm,),
        in_specs=[row_spec, pl.BlockSpec((None, d, d), lambda i: (layer, 0, 0)), row_spec,
                  pl.BlockSpec((1, d), lambda i: (0, 0))],
        out_specs=[row_spec, row_spec],
        out_shape=[jax.ShapeDtypeStruct((s, d), F32), jax.ShapeDtypeStruct((s, d), BF16)],
        compiler_params=_params("parallel"),
        name="out_proj_norm",
    )(merged, w_stack, res, g)


def _qkv_kernel(x_ref, w_ref, gq_ref, gk_ref, qo_ref, ko_ref, vo_ref):
    j = pl.program_id(1)
    acc = _dot(x_ref[...], w_ref[...])

    def heads(fn, out_ref):
        for h in range(N_GROUPS):
            out_ref[h] = fn(acc[:, h * GROUP_DIM:(h + 1) * GROUP_DIM]).astype(BF16)

    @pl.when(j == 0)
    def _():
        heads(lambda t: _rms(t) * gq_ref[...] * (GROUP_DIM ** -0.5 * LOG2E), qo_ref)

    @pl.when(j == 1)
    def _():
        heads(lambda t: _rms(t) * gk_ref[...], ko_ref)

    @pl.when(j == 2)
    def _():
        heads(lambda t: t, vo_ref)


def _qkv_proj(xn, w_stack, layer, gq, gk, tm):
    s, d = xn.shape
    hm_spec = pl.BlockSpec((N_GROUPS, tm, GROUP_DIM), lambda i, j: (0, i, 0))
    hm_shape = jax.ShapeDtypeStruct((N_GROUPS, s, GROUP_DIM), BF16)
    g_spec = pl.BlockSpec((1, GROUP_DIM), lambda i, j: (0, 0))
    return pl.pallas_call(
        _qkv_kernel,
        grid=(s // tm, 3),
        in_specs=[pl.BlockSpec((tm, d), lambda i, j: (i, 0)),
                  pl.BlockSpec((None, d, BRANCH_W), lambda i, j: (layer, 0, j)), g_spec, g_spec],
        out_specs=[hm_spec, hm_spec, hm_spec],
        out_shape=[hm_shape, hm_shape, hm_shape],
        compiler_params=_params("arbitrary", "arbitrary"),
        name="qkv_proj",
    )(xn, w_stack, gq, gk)


def _cast_at_first_row_tile(pairs):
    @pl.when(pl.program_id(1) == 0)
    def _():
        for w_ref, wb_s in pairs:
            wb_s[...] = w_ref[...].astype(BF16)


def _merge_kernel(xn_ref, ya_ref, yb_ref, yc_ref, yd_ref, wg_ref, bg_ref, wb_ref, o_ref, wg_s, wb_s):
    _cast_at_first_row_tile([(wg_ref, wg_s), (wb_ref, wb_s)])
    xn = xn_ref[...]
    acc = None
    for g, y_ref in enumerate((ya_ref, yb_ref, yc_ref, yd_ref)):
        gate = jax.nn.sigmoid(_dot(xn, wg_s[g]) + bg_ref[g:g + 1, :])
        term = gate * _dot(y_ref[...], wb_s[g])
        acc = term if acc is None else acc + term
    o_ref[...] = acc.astype(o_ref.dtype)


def _merge(xn, ys, wg_stack, bg_stack, wb_stack, layer, tm, tn):
    s, d = xn.shape
    w = ys[0].shape[1]
    y_spec = pl.BlockSpec((tm, w), lambda j, i: (i, 0))
    return pl.pallas_call(
        _merge_kernel,
        grid=(d // tn, s // tm),
        in_specs=[pl.BlockSpec((tm, d), lambda j, i: (i, 0)),
                  y_spec, y_spec, y_spec, y_spec,
                  pl.BlockSpec((None, N_BRANCH, d, tn), lambda j, i: (layer, 0, 0, j)),
                  pl.BlockSpec((None, N_BRANCH, tn), lambda j, i: (layer, 0, j)),
                  pl.BlockSpec((None, N_BRANCH, w, tn), lambda j, i: (layer, 0, 0, j))],
        out_specs=pl.BlockSpec((tm, tn), lambda j, i: (i, j)),
        out_shape=jax.ShapeDtypeStruct((s, d), BF16),
        scratch_shapes=[pltpu.VMEM((N_BRANCH, d, tn), BF16), pltpu.VMEM((N_BRANCH, w, tn), BF16)],
        compiler_params=_params("arbitrary", "arbitrary"),
        name="merge",
    )(xn, *ys, wg_stack, bg_stack, wb_stack)


def _ffn_up_kernel(x_ref, wg_ref, wu_ref, o_ref, wg_s, wu_s):
    _cast_at_first_row_tile([(wg_ref, wg_s), (wu_ref, wu_s)])
    x = x_ref[...]
    o_ref[...] = (jax.nn.silu(_dot(x, wg_s[...])) * _dot(x, wu_s[...])).astype(o_ref.dtype)


def _ffn_up(hn, wg_stack, wu_stack, layer, tm, tn):
    s, d = hn.shape
    f = wg_stack.shape[2]
    w_spec = pl.BlockSpec((None, d, tn), lambda j, i: (layer, 0, j))
    return pl.pallas_call(
        _ffn_up_kernel,
        grid=(f // tn, s // tm),
        in_specs=[pl.BlockSpec((tm, d), lambda j, i: (i, 0)), w_spec, w_spec],
        out_specs=pl.BlockSpec((tm, tn), lambda j, i: (i, j)),
        out_shape=jax.ShapeDtypeStruct((s, f), BF16),
        scratch_shapes=[pltpu.VMEM((d, tn), BF16), pltpu.VMEM((d, tn), BF16)],
        compiler_params=_params("arbitrary", "arbitrary"),
        name="ffn_up",
    )(hn, wg_stack, wu_stack)


def _col_spec(t, col):
    return pl.BlockSpec((t, BRANCH_W), lambda i: (i, col))


def _halo_spec(t, col):
    return pl.BlockSpec((8, BRANCH_W), lambda i: (jnp.maximum(i * (t // 8) - 1, 0), col))


def _full_spec(shape):
    nd = len(shape)
    return pl.BlockSpec(shape, lambda i: (0,) * nd)


def _shift_rows(x, prev8, j):
    xr = pltpu.roll(x, j, 0)
    pr = pltpu.roll(prev8, j, 0)
    rid = lax.broadcasted_iota(jnp.int32, pr.shape, 0)
    top = jnp.where(rid < j, pr, xr[:8])
    return jnp.concatenate([top, xr[8:]], axis=0)


def _causal_conv(x, prev8, w_ref, b_ref):
    acc = None
    for k in range(CONV_W):
        j = CONV_W - 1 - k
        xs = x if j == 0 else _shift_rows(x, prev8, j)
        term = xs * w_ref[k:k + 1, :]
        acc = term if acc is None else acc + term
    return acc + b_ref[...]


def _gmlp_kernel(u_ref, v_ref, gv_ref, ws_ref, bst_ref, o_ref):
    t = u_ref.shape[0]
    u = jax.nn.gelu(u_ref[...])
    v = (_rms(jax.nn.gelu(v_ref[...])) * gv_ref[...]).astype(BF16)
    row = lax.broadcasted_iota(jnp.int32, (CHUNK, CHUNK), 0)
    col = lax.broadcasted_iota(jnp.int32, (CHUNK, CHUNK), 1)
    for g in range(N_GROUPS):
        gs = slice(g * GROUP_DIM, (g + 1) * GROUP_DIM)
        ws = jnp.where(col <= row, ws_ref[g], 0.0).astype(BF16)
        bcol = bst_ref[:, g:g + 1]
        for c in range(t // CHUNK):
            cs = slice(c * CHUNK, (c + 1) * CHUNK)
            sp = _dot(ws, v[cs, gs]) + bcol
            o_ref[cs, gs] = (u[cs, gs] * sp).astype(o_ref.dtype)


def _gmlp(proj, gv, ws, bst, t):
    s = proj.shape[0]
    return pl.pallas_call(
        _gmlp_kernel,
        grid=(s // t,),
        in_specs=[_col_spec(t, COL_A_U), _col_spec(t, COL_A_V),
                  _full_spec((1, BRANCH_W)), _full_spec((N_GROUPS, CHUNK, CHUNK)),
                  _full_spec((CHUNK, N_GROUPS))],
        out_specs=pl.BlockSpec((t, BRANCH_W), lambda i: (i, 0)),
        out_shape=jax.ShapeDtypeStruct((s, BRANCH_W), BF16),
        compiler_params=_params("parallel"),
        name="gmlp",
    )(proj, proj, gv, ws, bst)


def _lin_scan(a, b):
    t = a.shape[0]
    rid = lax.broadcasted_iota(jnp.int32, a.shape, 0)
    d = 1
    while d < t:
        keep = rid >= d
        a_sh = jnp.where(keep, pltpu.roll(a, d, 0), 1.0)
        b_sh = jnp.where(keep, pltpu.roll(b, d, 0), 0.0)
        b = a * b_sh + b
        a = a * a_sh
        d *= 2
    return a, b


def _rglru_kernel(x_ref, xp_ref, gt_ref, cw_ref, cb_ref, wr_ref, br_ref, wi_ref, bi_ref, lam_ref,
                  o_ref, h_s):
    i = pl.program_id(0)

    @pl.when(i == 0)
    def _():
        h_s[...] = jnp.zeros_like(h_s)

    prev = jnp.where(i == 0, 0.0, xp_ref[...])
    xc = _causal_conv(x_ref[...], prev, cw_ref, cb_ref)
    xcb = xc.astype(BF16)
    r_parts, i_parts = [], []
    for g in range(N_GROUPS):
        gs = slice(g * GROUP_DIM, (g + 1) * GROUP_DIM)
        r_parts.append(_dot(xcb[:, gs], wr_ref[g]))
        i_parts.append(_dot(xcb[:, gs], wi_ref[g]))
    r = jax.nn.sigmoid(jnp.concatenate(r_parts, axis=1) + br_ref[...])
    ig = jax.nn.sigmoid(jnp.concatenate(i_parts, axis=1) + bi_ref[...])
    log_a = LRU_C * r * _log_sigmoid(lam_ref[...])
    a = jnp.exp(log_a)
    bx = jnp.sqrt(-jnp.tanh(log_a) * (a * a + 1.0)) * (ig * xc)
    a_cum, h_loc = _lin_scan(a, bx)
    h = h_loc + a_cum * h_s[0:1, :]
    t = h.shape[0]
    h_s[...] = jnp.broadcast_to(h[t - 1:t, :], h_s.shape)
    o_ref[...] = (h * jax.nn.gelu(gt_ref[...])).astype(o_ref.dtype)


def _rglru(proj, cw, cb, wr, br, wi, bi, lam, t):
    s = proj.shape[0]
    return pl.pallas_call(
        _rglru_kernel,
        grid=(s // t,),
        in_specs=[_col_spec(t, COL_B_X), _halo_spec(t, COL_B_X), _col_spec(t, COL_B_G),
                  _full_spec((CONV_W, BRANCH_W)), _full_spec((1, BRANCH_W)),
                  _full_spec((N_GROUPS, GROUP_DIM, GROUP_DIM)), _full_spec((1, BRANCH_W)),
                  _full_spec((N_GROUPS, GROUP_DIM, GROUP_DIM)), _full_spec((1, BRANCH_W)),
                  _full_spec((1, BRANCH_W))],
        out_specs=pl.BlockSpec((t, BRANCH_W), lambda i: (i, 0)),
        out_shape=jax.ShapeDtypeStruct((s, BRANCH_W), BF16),
        scratch_shapes=[pltpu.VMEM((8, BRANCH_W), F32)],
        compiler_params=_params("arbitrary"),
        name="rglru",
    )(proj, proj, proj, cw, cb, wr, br, wi, bi, lam)


def _mlstm_kernel(q_ref, qp_ref, k_ref, kp_ref, v_ref, og_ref, ifc_ref, ifr_ref,
                  cwq_ref, cbq_ref, cwk_ref, cbk_ref, bifc_ref, bifr_ref, gh_ref,
                  y_ref, ct_s, n_s, m_s):
    i = pl.program_id(0)

    @pl.when(i == 0)
    def _():
        ct_s[...] = jnp.zeros_like(ct_s)
        n_s[...] = jnp.zeros_like(n_s)
        m_s[...] = jnp.zeros_like(m_s)

    first = i == 0
    q = jax.nn.silu(_causal_conv(q_ref[...], jnp.where(first, 0.0, qp_ref[...]), cwq_ref, cbq_ref))
    k = jax.nn.silu(_causal_conv(k_ref[...], jnp.where(first, 0.0, kp_ref[...]), cwk_ref, cbk_ref))
    k = k * (GROUP_DIM ** -0.5)
    v = v_ref[...]
    og = og_ref[...]

    row = lax.broadcasted_iota(jnp.int32, (CHUNK, CHUNK), 0)
    col = lax.broadcasted_iota(jnp.int32, (CHUNK, CHUNK), 1)
    tril = col <= row
    tril_b = jnp.where(tril, 1.0, 0.0).astype(BF16)
    triu_b = jnp.where(row <= col, 1.0, 0.0).astype(BF16)
    state = [(ct_s[h], n_s[h:h + 1, :], m_s[h:h + 1, 0:1]) for h in range(N_GROUPS)]

    for c in range(q.shape[0] // CHUNK):
        cs = slice(c * CHUNK, (c + 1) * CHUNK)
        ifc = ifc_ref[cs, :] + bifc_ref[...]
        ifr = ifr_ref[:, cs] + bifr_ref[...]
        bcc = sum(_dot(tril_b, piece) for piece in _split3(_log_sigmoid(ifc)))
        bcr = sum(_dot(piece, triu_b) for piece in _split3(_log_sigmoid(ifr)))

        for h in range(N_GROUPS):
            hs = slice(h * GROUP_DIM, (h + 1) * GROUP_DIM)
            qh, kh, vh = q[cs, hs], k[cs, hs], v[cs, hs]
            ic_row, ic_col = ifr[h:h + 1, :], ifc[:, h:h + 1]
            bc_row, bc_col = bcr[N_GROUPS + h:N_GROUPS + h + 1, :], bcc[:, N_GROUPS + h:N_GROUPS + h + 1]
            ct, n_row, m_st = state[h]

            dmat = jnp.where(tril, bc_col - bc_row + ic_row, -jnp.inf)
            m_inter = bc_col + m_st
            m_t = jnp.maximum(m_inter, jnp.max(dmat, axis=-1, keepdims=True))
            kt = kh.T
            qb, vb = qh.astype(BF16), vh.astype(BF16)
            p = _dot(qb, kt.astype(BF16)) * jnp.exp(dmat - m_t)
            sc = jnp.exp(m_inter - m_t)
            num = _dot(p.astype(BF16), vb) + sc * _dot(qb, ct.astype(BF16))
            den = jnp.sum(p, axis=-1, keepdims=True) + sc * jnp.sum(qh * n_row, axis=-1, keepdims=True)
            hh = num / jnp.maximum(jnp.abs(den), jnp.exp(-m_t))

            b_last = bc_row[:, CHUNK - 1:CHUNK]
            g_row = b_last - bc_row + ic_row
            g_col = b_last - bc_col + ic_col
            m_new = jnp.maximum(b_last + m_st, jnp.max(g_row, axis=-1, keepdims=True))
            w_row = jnp.exp(g_row - m_new)
            w_col = jnp.exp(g_col - m_new)
            decay = jnp.exp(b_last + m_st - m_new)
            state[h] = (decay * ct + _dot((kt * w_row).astype(BF16), vb),
                        decay * n_row + jnp.sum(kh * w_col, axis=0, keepdims=True),
                        m_new)

            hn = _rms(hh) * gh_ref[:, hs]
            y_ref[cs, hs] = (jax.nn.sigmoid(og[cs, hs]) * hn).astype(y_ref.dtype)

    for h in range(N_GROUPS):
        ct_s[h] = state[h][0]
        n_s[h:h + 1, :] = state[h][1]
        m_s[h:h + 1, :] = jnp.broadcast_to(state[h][2], (1, GROUP_DIM))


def _mlstm(proj, ifc, ifr, cwq, cbq, cwk, cbk, bifc, bifr, gh, t):
    s = proj.shape[0]
    return pl.pallas_call(
        _mlstm_kernel,
        grid=(s // t,),
        in_specs=[_col_spec(t, COL_C_Q), _halo_spec(t, COL_C_Q),
                  _col_spec(t, COL_C_K), _halo_spec(t, COL_C_K),
                  _col_spec(t, COL_C_V), _col_spec(t, COL_C_O),
                  pl.BlockSpec((t, IF_PAD), lambda i: (i, 0)),
                  pl.BlockSpec((IF_ROWS, t), lambda i: (0, i)),
                  _full_spec((CONV_W, BRANCH_W)), _full_spec((1, BRANCH_W)),
                  _full_spec((CONV_W, BRANCH_W)), _full_spec((1, BRANCH_W)),
                  _full_spec((1, IF_PAD)), _full_spec((IF_ROWS, 1)),
                  _full_spec((1, BRANCH_W))],
        out_specs=pl.BlockSpec((t, BRANCH_W), lambda i: (i, 0)),
        out_shape=jax.ShapeDtypeStruct((s, BRANCH_W), BF16),
        scratch_shapes=[pltpu.VMEM((N_GROUPS, GROUP_DIM, GROUP_DIM), F32),
                        pltpu.VMEM((8, GROUP_DIM), F32),
                        pltpu.VMEM((8, GROUP_DIM), F32)],
        compiler_params=_params("arbitrary"),
        name="mlstm",
    )(proj, proj, proj, proj, proj, proj, ifc, ifr, cwq, cbq, cwk, cbk, bifc, bifr, gh)


def _sb_kernel(q_ref, k_ref, v_ref, o_ref, *, tq, sub):
    qi = pl.program_id(1)
    nsub = tq // sub
    q = q_ref[0]
    krow = lax.broadcasted_iota(jnp.int32, (sub, sub), 0)
    kcol = lax.broadcasted_iota(jnp.int32, (sub, sub), 1)
    from_j = jnp.where(krow >= kcol, 1.0, 0.0).astype(BF16)

    def piece(qa, start, strict, run):
        kb = k_ref[0, pl.ds(start, sub), :]
        z = _dot_nt(qa, kb)
        neg_abs = lax.bitcast_convert_type(
            lax.bitcast_convert_type(z, jnp.uint32) | jnp.uint32(0x80000000), F32)
        sp = jnp.maximum(z, 0.0) + jnp.log(1.0 + jnp.exp2(neg_abs)) * LOG2E
        if strict is not None:
            sp = jnp.where(strict, sp, 0.0)
        att = jnp.exp2(z - _dot(sp.astype(BF16), from_j) - jnp.concatenate([run] * (sub // GROUP_DIM), axis=1))
        if strict is not None:
            att = jnp.where(strict, att, 0.0)
        return att.astype(BF16), run + jnp.sum(sp, axis=-1, keepdims=True)

    acc = jnp.zeros((tq, GROUP_DIM), F32)
    run = jnp.zeros((tq, GROUP_DIM), F32)
    for b in range(nsub - 1, -1, -1):
        r0 = b * sub
        qpos = lax.broadcasted_iota(jnp.int32, (tq - r0, sub), 0)
        kpos = lax.broadcasted_iota(jnp.int32, (tq - r0, sub), 1)
        start = pl.multiple_of(qi * tq + r0, sub)
        att, run_b = piece(q[r0:], start, kpos < qpos, run[r0:])
        acc_b = acc[r0:] + _dot(att, v_ref[0, pl.ds(start, sub), :])
        acc = acc_b if b == 0 else jnp.concatenate([acc[:r0], acc_b], axis=0)
        run = run_b if b == 0 else jnp.concatenate([run[:r0], run_b], axis=0)

    def tile(j, carry):
        acc, run = carry
        base = (qi - 1 - j) * tq
        atts = []
        for b in range(nsub - 1, -1, -1):
            att, run = piece(q, pl.multiple_of(base + b * sub, sub), None, run)
            atts.append(att)
        vb = v_ref[0, pl.ds(pl.multiple_of(base, tq), tq), :]
        return acc + _dot(jnp.concatenate(atts[::-1], axis=1), vb), run

    acc, run = lax.fori_loop(0, qi, tile, (acc, run))
    o_ref[...] = acc.astype(o_ref.dtype)


def _sb_attention(qh, kh, vh, tq):
    _, s, _ = qh.shape
    return pl.pallas_call(
        functools.partial(_sb_kernel, tq=tq, sub=min(tq, 256)),
        grid=(N_GROUPS, s // tq),
        in_specs=[pl.BlockSpec((1, tq, GROUP_DIM), lambda h, i: (h, i, 0)),
                  pl.BlockSpec((1, s, GROUP_DIM), lambda h, i: (h, 0, 0)),
                  pl.BlockSpec((1, s, GROUP_DIM), lambda h, i: (h, 0, 0))],
        out_specs=pl.BlockSpec((tq, GROUP_DIM), lambda h, i: (i, h)),
        out_shape=jax.ShapeDtypeStruct((s, BRANCH_W), BF16),
        compiler_params=_params("parallel", "parallel"),
        name="sb_attn",
    )(qh, kh, vh)


def _tile(s, pref):
    return min(s, pref)


def _layer(x, p):
    s = x.shape[0]
    l, st = p["layer"], p["stacks"]
    xn, ifc, ifr = _norm_if(x, p["norm_mix"], p["w_if"], p["w_if_t"], _tile(s, 512))
    proj = _matmul(xn, st["w_in_abc_bf16"], l, F32, _tile(s, 1024), 1024, "in_proj")
    qh, kh, vh = _qkv_proj(xn, st["w_in_d_bf16"], l, p["sb_norm_q"], p["sb_norm_k"], _tile(s, 1024))

    y_a = _gmlp(proj, p["gm_norm_v"], p["gm_w_s"], p["gm_b_s_t"], _tile(s, 512))
    y_b = _rglru(proj, p["lru_conv_w"], p["lru_conv_b"], p["lru_w_r"], p["lru_b_r"],
                 p["lru_w_i"], p["lru_b_i"], p["lru_lambda"], _tile(s, 512))
    y_c = _mlstm(proj, ifc, ifr, p["ml_conv_w_q"], p["ml_conv_b_q"], p["ml_conv_w_k"], p["ml_conv_b_k"],
                 p["ml_b_if_c"], p["ml_b_if_r"], p["ml_norm_h"], CHUNK)
    y_d = _sb_attention(qh, kh, vh, _tile(s, 2048))

    merged = _merge(xn, (y_a, y_b, y_c, y_d), st["w_gate"], st["b_gate"], st["w_branch"], l, _tile(s, 1024), 256)
    x, hn = _out_proj_norm(merged, st["w_out_bf16"], l, x, p["norm_ffn"], _tile(s, 512))
    hid = _ffn_up(hn, st["w_ffn_gate"], st["w_ffn_up"], l, _tile(s, 1024), 512)
    return _matmul_residual(hid, st["w_ffn_down_bf16"], l, x, _tile(s, 1024), 512, "ffn_down")


def _prepare_layer(l, stacks, norm_mix, w_in, gm_norm_v, gm_w_s, gm_b_s, lru_conv_w, lru_conv_b, lru_w_r, lru_b_r,
                   lru_w_i, lru_b_i, lru_lambda, ml_conv_w, ml_conv_b, ml_b_i, ml_b_f, ml_norm_h, sb_norm_q,
                   sb_norm_k, w_branch, w_gate, b_gate, w_out, norm_ffn, w_ffn_gate, w_ffn_up, w_ffn_down):
    w = BRANCH_W
    n_main = 8 * w
    w_if = w_in[l, :, n_main:n_main + 2 * N_GROUPS].astype(BF16)
    w_if_c = jnp.pad(w_if, ((0, 0), (0, IF_PAD - 2 * N_GROUPS)))
    w_if_r = jnp.pad(w_if.T, ((0, IF_ROWS - 2 * N_GROUPS), (0, 0)))
    b_if = jnp.concatenate([ml_b_i[l], ml_b_f[l]])
    return {
        "layer": l, "stacks": stacks,
        "norm_mix": norm_mix[l][None, :],
        "w_if": w_if_c, "w_if_t": w_if_r,
        "gm_norm_v": gm_norm_v[l][None, :],
        "gm_w_s": gm_w_s[l],
        "gm_b_s_t": gm_b_s[l].T,
        "lru_conv_w": lru_conv_w[l], "lru_conv_b": lru_conv_b[l][None, :],
        "lru_w_r": lru_w_r[l].astype(BF16), "lru_b_r": lru_b_r[l][None, :],
        "lru_w_i": lru_w_i[l].astype(BF16), "lru_b_i": lru_b_i[l][None, :],
        "lru_lambda": lru_lambda[l][None, :],
        "ml_conv_w_q": ml_conv_w[l][:, :w], "ml_conv_b_q": ml_conv_b[l][None, :w],
        "ml_conv_w_k": ml_conv_w[l][:, w:], "ml_conv_b_k": ml_conv_b[l][None, w:],
        "ml_b_if_c": jnp.pad(b_if, (0, IF_PAD - 2 * N_GROUPS))[None, :],
        "ml_b_if_r": jnp.pad(b_if, (0, IF_ROWS - 2 * N_GROUPS))[:, None],
        "ml_norm_h": ml_norm_h[l].reshape(1, w),
        "sb_norm_q": sb_norm_q[l][None, :], "sb_norm_k": sb_norm_k[l][None, :],
        "norm_ffn": norm_ffn[l][None, :],
    }


def kernel(x, norm_mix, w_in, gm_norm_v, gm_w_s, gm_b_s, lru_conv_w, lru_conv_b, lru_w_r, lru_b_r, lru_w_i, lru_b_i, lru_lambda, ml_conv_w, ml_conv_b, ml_b_i, ml_b_f, ml_norm_h, sb_norm_q, sb_norm_k, w_branch, w_gate, b_gate, w_out, norm_ffn, w_ffn_gate, w_ffn_up, w_ffn_down):
    b, s, d = x.shape
    n_abc = 8 * BRANCH_W
    stacks = {"w_in_abc_bf16": w_in[:, :, :n_abc].astype(BF16),
              "w_in_d_bf16": w_in[:, :, n_abc + 2 * N_GROUPS:].astype(BF16),
              "w_gate": w_gate, "b_gate": b_gate, "w_branch": w_branch, "w_out_bf16": w_out.astype(BF16),
              "w_ffn_gate": w_ffn_gate, "w_ffn_up": w_ffn_up, "w_ffn_down_bf16": w_ffn_down.astype(BF16)}
    outs = []
    for bi in range(b):
        xb = x.reshape(s, d) if b == 1 else x[bi]
        for l in range(norm_mix.shape[0]):
            p = _prepare_layer(l, stacks, norm_mix, w_in, gm_norm_v, gm_w_s, gm_b_s, lru_conv_w, lru_conv_b,
                               lru_w_r, lru_b_r, lru_w_i, lru_b_i, lru_lambda, ml_conv_w, ml_conv_b, ml_b_i,
                               ml_b_f, ml_norm_h, sb_norm_q, sb_norm_k, w_branch, w_gate, b_gate, w_out,
                               norm_ffn, w_ffn_gate, w_ffn_up, w_ffn_down)
            xb = _layer(xb, p)
        outs.append(xb)
    return outs[0].reshape(b, s, d) if b == 1 else jnp.stack(outs, axis=0)
```
